```python
import math
import jax
import jax.numpy as jnp
from jax import lax
import numpy as np

D_MODEL = 2048
BATCH = 2
SEQ = 4096
DEPTH = 2
DEC_BATCH = 8
DEC_SEQ = 4
PAST_LEN = 16384
PAGE_SIZE = 128

N_MIXERS = 2
N_A_LAYERS = (DEPTH + 1) // 2
N_B_LAYERS = DEPTH // 2
H_A = 16
DH_A = D_MODEL // H_A
W_A = H_A * DH_A
Q_BLOCK = 128
SB_SCALE = DH_A ** -0.5
H_B = 4
DV_B = D_MODEL // H_B
DK_B = DV_B // 2
QK_W = H_B * DK_B
V_W = H_B * DV_B
IN_B = 2 * QK_W + 2 * V_W + 2 * H_B
MLSTM_CHUNK = 128
D_FF = 256 * ((8 * D_MODEL // 3 + 255) // 256)
CONV_W = 3
ALPHA = (2 * DEPTH) ** 0.25
BETA = (8 * DEPTH) ** -0.25
LN_EPS = 1e-5
HEAD_EPS = 1e-6

kernel_name = 'stickbreak_mlstm_convffn_deepnorm_step'


def layer_norm(x, g, b):
    xf = x.astype(jnp.float32)
    mu = jnp.mean(xf, axis=-1, keepdims=True)
    var = jnp.mean(jnp.square(xf - mu), axis=-1, keepdims=True)
    return ((xf - mu) * lax.rsqrt(var + LN_EPS) * g + b).astype(x.dtype)


def sb_weights(z, mask):
    log_beta = jax.nn.log_sigmoid(z)
    log_keep = jnp.where(mask, jax.nn.log_sigmoid(-z), 0.0)
    suffix = lax.cumsum(log_keep, axis=z.ndim - 1, reverse=True) - log_keep
    return jnp.where(mask, jnp.exp(log_beta + suffix), 0.0)


def sb_prompt(x, w_in, w_out, bias):
    bx, t, _ = x.shape
    q, k, v = jnp.split(x @ w_in, 3, axis=-1)
    k_rows = k.reshape(bx, t, H_A, DH_A)
    v_rows = v.reshape(bx, t, H_A, DH_A)
    n_blk = t // Q_BLOCK
    qb = q.reshape(bx, n_blk, Q_BLOCK, H_A, DH_A).transpose(1, 0, 3, 2, 4)
    kt = k_rows.transpose(0, 2, 1, 3)
    vt = v_rows.transpose(0, 2, 1, 3)
    k_pos = jnp.arange(t)
    hb = bias.astype(jnp.float32)[None, :, None, None]

    def block(args):
        q_blk, q0 = args
        z = jnp.einsum('bhqd,bhkd->bhqk', q_blk, kt).astype(jnp.float32) * SB_SCALE + hb
        q_pos = q0 + jnp.arange(Q_BLOCK)
        a = sb_weights(z, k_pos[None, :] < q_pos[:, None])
        return jnp.einsum('bhqk,bhkd->bhqd', a.astype(vt.dtype), vt)

    o = lax.map(block, (qb, jnp.arange(n_blk) * Q_BLOCK))
    o = o.transpose(1, 0, 3, 2, 4).reshape(bx, t, W_A)
    return o.astype(x.dtype) @ w_out, k_rows, v_rows


def sb_sample(x, cache_k, cache_v, layer, page_table, w_in, w_out, bias):
    bx, t, _ = x.shape
    q, k, v = jnp.split(x @ w_in, 3, axis=-1)
    q = q.reshape(bx, t, H_A, DH_A)
    k = k.reshape(bx, t, H_A, DH_A)
    v = v.reshape(bx, t, H_A, DH_A)
    k_past = cache_k[layer][page_table].reshape(bx, -1, H_A, DH_A)
    v_past = cache_v[layer][page_table].reshape(bx, -1, H_A, DH_A)
    p = k_past.shape[1]
    hb = bias.astype(jnp.float32)[None, :, None, None]
    z = jnp.concatenate([jnp.einsum('bqhd,bkhd->bhqk', q, k_past),
                         jnp.einsum('bqhd,bkhd->bhqk', q, k)], axis=-1).astype(jnp.float32) * SB_SCALE + hb
    mask = jnp.concatenate([jnp.ones((t, p), bool), jnp.tril(jnp.ones((t, t), bool), -1)], axis=1)
    a = sb_weights(z, mask).astype(v.dtype)
    o = (jnp.einsum('bhqk,bkhd->bqhd', a[..., :p], v_past)
         + jnp.einsum('bhqk,bkhd->bqhd', a[..., p:], v))
    return o.reshape(bx, t, W_A).astype(x.dtype) @ w_out, k, v


def mlstm_chunk(carry, xs):
    c, n, m = carry
    q, k, v, lf, ig = xs
    l = q.shape[2]
    b = jnp.cumsum(lf, axis=-1)
    causal = jnp.tril(jnp.ones((l, l), bool))
    d = jnp.where(causal, b[..., :, None] - b[..., None, :] + ig[..., None, :], -jnp.inf)
    inter = b + m[..., None]
    m_t = jnp.maximum(inter, jnp.max(d, axis=-1))
    dw = jnp.exp(d - m_t[..., None])
    wi = jnp.exp(inter - m_t)
    s = jnp.einsum('bhtd,bhsd->bhts', q, k) * dw
    num = wi[..., None] * jnp.einsum('bhvd,bhtd->bhtv', c, q) + jnp.einsum('bhts,bhsv->bhtv', s, v)
    den = wi * jnp.einsum('bhd,bhtd->bht', n, q) + jnp.sum(s, axis=-1)
    h = num / jnp.maximum(jnp.abs(den), jnp.exp(-m_t))[..., None]
    b_last = b[..., -1]
    g = b_last[..., None] - b + ig
    m_new = jnp.maximum(b_last + m, jnp.max(g, axis=-1))
    wc = jnp.exp(b_last + m - m_new)
    ws = jnp.exp(g - m_new[..., None])
    c_new = wc[..., None, None] * c + jnp.einsum('bhs,bhsv,bhsd->bhvd', ws, v, k)
    n_new = wc[..., None] * n + jnp.einsum('bhs,bhsd->bhd', ws, k)
    return (c_new, n_new, m_new), h


def mlstm_mixer(x, c0, n0, m0, w_in, b_gate, g_head, w_out):
    bx, t, _ = x.shape
    zp = x @ w_in
    q = zp[..., :QK_W]
    k = zp[..., QK_W:2 * QK_W]
    v = zp[..., 2 * QK_W:2 * QK_W + V_W]
    o = zp[..., 2 * QK_W + V_W:2 * QK_W + 2 * V_W]
    gates = (zp[..., 2 * QK_W + 2 * V_W:] + b_gate).astype(jnp.float32)
    ig = gates[..., :H_B].transpose(0, 2, 1)
    lf = jax.nn.log_sigmoid(gates[..., H_B:]).transpose(0, 2, 1)

    def heads(a, dh):
        return a.reshape(bx, t, H_B, dh).transpose(0, 2, 1, 3).astype(jnp.float32)

    qh = heads(q, DK_B)
    kh = heads(k, DK_B) * (DK_B ** -0.5)
    vh = heads(v, DV_B)
    l = MLSTM_CHUNK if t % MLSTM_CHUNK == 0 else t
    nc = t // l

    def chunks(a):
        return jnp.moveaxis(a.reshape(bx, H_B, nc, l, *a.shape[3:]), 2, 0)

    carry0 = (c0.astype(jnp.float32), n0.astype(jnp.float32), m0.astype(jnp.float32))
    (c, n, m), h = lax.scan(mlstm_chunk, carry0, (chunks(qh), chunks(kh), chunks(vh), chunks(lf), chunks(ig)))
    h = h.transpose(1, 0, 3, 2, 4).reshape(bx, t, H_B, DV_B)
    h = h * lax.rsqrt(jnp.mean(jnp.square(h), axis=-1, keepdims=True) + HEAD_EPS)
    h = h.reshape(bx, t, V_W) * g_head * jax.nn.sigmoid(o.astype(jnp.float32))
    return h.astype(x.dtype) @ w_out, c, n, m


def conv_ffn(x, prefix, w_up, conv_w, conv_b, w_down):
    t = x.shape[1]
    u = x @ w_up
    up = jnp.concatenate([prefix.astype(u.dtype), u], axis=1)
    c = conv_b + conv_w[0] * up[:, 0:t]
    for j in range(1, CONV_W):
        c = c + conv_w[j] * up[:, j:j + t]
    gate, val = jnp.split(c, 2, axis=-1)
    y = (jax.nn.silu(gate) * val) @ w_down
    return y, up[:, t:]


def setup_inputs(seed: int = 0) -> dict:
    key = jax.random.key(seed)
    ks = jax.random.split(key, 26)
    n_pages = PAST_LEN // PAGE_SIZE
    n_used = DEC_BATCH * n_pages
    n_pool = n_used + max(1, n_used // 4)
    nrm = jax.random.normal
    f32 = jnp.float32
    page_table = jax.random.permutation(ks[3], n_pool)[:n_used].reshape(DEC_BATCH, n_pages).astype(jnp.int32)
    f_bias = jnp.linspace(3.0, 6.0, H_B)[None, :] + 0.1 * nrm(ks[10], (N_B_LAYERS, H_B), f32)
    i_bias = 0.1 * nrm(ks[11], (N_B_LAYERS, H_B), f32)
    sb_bias = jnp.linspace(-4.0, -10.0, H_A)[None, :] + 0.1 * nrm(ks[24], (N_A_LAYERS, H_A), f32)
    return {
        'x_prompt': nrm(ks[0], (BATCH, SEQ, D_MODEL), f32),
        'x_sample': nrm(ks[1], (DEC_BATCH, DEC_SEQ, D_MODEL), f32),
        'cache_k': nrm(ks[2], (N_A_LAYERS, n_pool, PAGE_SIZE, H_A, DH_A), f32),
        'cache_v': nrm(ks[4], (N_A_LAYERS, n_pool, PAGE_SIZE, H_A, DH_A), f32),
        'page_table': page_table,
        'state_C': 0.3 * nrm(ks[5], (N_B_LAYERS, DEC_BATCH, H_B, DV_B, DK_B), f32),
        'state_n': 0.3 * nrm(ks[6], (N_B_LAYERS, DEC_BATCH, H_B, DK_B), f32),
        'state_m': 0.5 * nrm(ks[7], (N_B_LAYERS, DEC_BATCH, H_B), f32),
        'state_conv': nrm(ks[8], (DEPTH, DEC_BATCH, CONV_W - 1, 2 * D_FF), f32),
        'w_in_a': nrm(ks[9], (N_A_LAYERS, D_MODEL, 3 * W_A), f32) * D_MODEL ** -0.5,
        'w_out_a': nrm(ks[12], (N_A_LAYERS, W_A, D_MODEL), f32) * (W_A ** -0.5 * BETA),
        'sb_bias': sb_bias,
        'w_in_b': nrm(ks[13], (N_B_LAYERS, D_MODEL, IN_B), f32) * D_MODEL ** -0.5,
        'b_gate_b': jnp.concatenate([i_bias, f_bias], axis=-1),
        'g_head_b': 1.0 + 0.02 * nrm(ks[14], (N_B_LAYERS, V_W), f32),
        'w_out_b': nrm(ks[15], (N_B_LAYERS, V_W, D_MODEL), f32) * (V_W ** -0.5 * BETA),
        'w_up': nrm(ks[16], (DEPTH, D_MODEL, 2 * D_FF), f32) * D_MODEL ** -0.5,
        'conv_w': nrm(ks[17], (DEPTH, CONV_W, 2 * D_FF), f32) * CONV_W ** -0.5,
        'conv_b': 0.02 * nrm(ks[18], (DEPTH, 2 * D_FF), f32),
        'w_down': nrm(ks[19], (DEPTH, D_FF, D_MODEL), f32) * (D_FF ** -0.5 * BETA),
        'ln1_g': 1.0 + 0.02 * nrm(ks[20], (DEPTH, D_MODEL), f32),
        'ln1_b': 0.02 * nrm(ks[21], (DEPTH, D_MODEL), f32),
        'ln2_g': 1.0 + 0.02 * nrm(ks[22], (DEPTH, D_MODEL), f32),
        'ln2_b': 0.02 * nrm(ks[23], (DEPTH, D_MODEL), f32),
    }


def reference(x_prompt, x_sample, cache_k, cache_v, page_table, state_C, state_n, state_m, state_conv,
              w_in_a, w_out_a, sb_bias, w_in_b, b_gate_b, g_head_b, w_out_b, w_up, conv_w, conv_b, w_down,
              ln1_g, ln1_b, ln2_g, ln2_b):
    def trunk(x, sample):
        bx = x.shape[0]
        k_new, v_new, c_new, n_new, m_new, conv_new = [], [], [], [], [], []
        for i in range(DEPTH):
            j = i // N_MIXERS
            if i % N_MIXERS == 0:
                if sample:
                    mix, kr, vr = sb_sample(x, cache_k, cache_v, j, page_table, w_in_a[j], w_out_a[j], sb_bias[j])
                else:
                    mix, kr, vr = sb_prompt(x, w_in_a[j], w_out_a[j], sb_bias[j])
                k_new.append(kr)
                v_new.append(vr)
            else:
                if sample:
                    c0, n0, m0 = state_C[j], state_n[j], state_m[j]
                else:
                    c0 = jnp.zeros((bx, H_B, DV_B, DK_B), jnp.float32)
                    n0 = jnp.zeros((bx, H_B, DK_B), jnp.float32)
                    m0 = jnp.zeros((bx, H_B), jnp.float32)
                mix, c, n, m = mlstm_mixer(x, c0, n0, m0, w_in_b[j], b_gate_b[j], g_head_b[j], w_out_b[j])
                c_new.append(c)
                n_new.append(n)
                m_new.append(m)
            x = layer_norm(ALPHA * x + mix, ln1_g[i], ln1_b[i])
            prefix = state_conv[i] if sample else jnp.zeros((bx, CONV_W - 1, 2 * D_FF), x.dtype)
            f, buf = conv_ffn(x, prefix, w_up[i], conv_w[i], conv_b[i], w_down[i])
            conv_new.append(buf)
            x = layer_norm(ALPHA * x + f, ln2_g[i], ln2_b[i])
        return (x, jnp.stack(k_new), jnp.stack(v_new), jnp.stack(c_new), jnp.stack(n_new),
                jnp.stack(m_new), jnp.stack(conv_new))

    y_prompt, k_p, v_p, c_p, n_p, m_p, conv_p = trunk(x_prompt, False)
    y_sample, k_s, v_s, c_s, n_s, m_s, conv_s = trunk(x_sample, True)
    return (y_prompt, y_sample, k_p, v_p, k_s, v_s, c_p, n_p, m_p, c_s, n_s, m_s, conv_p, conv_s)
```

```python
import functools

import jax
import jax.numpy as jnp
from jax import lax
from jax.experimental import pallas as pl
from jax.experimental.pallas import tpu as pltpu

F32 = jnp.float32
BF16 = jnp.bfloat16

LN_EPS = 1e-5
HEAD_EPS = 1e-6
CONV_W = 3
MLSTM_CHUNK = 128
N_MIXERS = 2

SUBLANES = 8
LANES = 128
MXU_DIM = 256
VMEM_LIMIT_BYTES = 56 * 1024 * 1024

TM_PROMPT = (1024, 512, 256, 128)
TM_LN = (512, 256, 128)
TM_DOWN = (256, 128)
TN_FF = (512, 256, 128)
TN_QKV = 256
T_ATTN = (MXU_DIM, LANES)


def _params(*semantics):
    return pltpu.CompilerParams(dimension_semantics=semantics, vmem_limit_bytes=VMEM_LIMIT_BYTES)


def _dot(a, b):
    return jnp.dot(a, b, preferred_element_type=F32)


def _dot_nt(a, b):
    return lax.dot_general(a, b, (((1,), (1,)), ((), ())), preferred_element_type=F32)


def _dot_tn(a, b):
    return lax.dot_general(a, b, (((0,), (0,)), ((), ())), preferred_element_type=F32)


def _split_dot(x, t, n_terms, left=False):
    acc = None
    rem = x
    for i in range(n_terms):
        part = rem.astype(BF16)
        term = _dot(t, part) if left else _dot(part, t)
        acc = term if acc is None else acc + term
        if i + 1 < n_terms:
            rem = rem - part.astype(F32)
    return acc


def _softplus(z):
    return jnp.maximum(z, 0.0) + jnp.log1p(jnp.exp(-jnp.abs(z)))


def _layer_norm_rows(z, g, b):
    mu = jnp.mean(z, axis=-1, keepdims=True)
    zc = z - mu
    var = jnp.mean(zc * zc, axis=-1, keepdims=True)
    return zc * lax.rsqrt(var + LN_EPS) * g + b


def _qkv_kernel(x_ref, wq_ref, wk_ref, wv_ref, qb_ref, kb_ref, vb_ref, kf_ref, vf_ref, xb_ref):
    @pl.when(pl.program_id(1) == 0)
    def _():
        xb_ref[...] = x_ref[...].astype(BF16)

    xb = xb_ref[...]
    qb_ref[...] = _dot(xb, wq_ref[...]).astype(BF16)
    k = _dot(xb, wk_ref[...])
    kf_ref[...] = k
    kb_ref[...] = k.astype(BF16)
    v = _dot(xb, wv_ref[...])
    vf_ref[...] = v
    vb_ref[...] = v.astype(BF16)


def _qkv_proj(x, w, tm, tn):
    m, d = x.shape
    wa = w.shape[1] // 3
    nj = wa // tn
    row = lambda i, j: (i, j)
    out_bf = jax.ShapeDtypeStruct((m, wa), BF16)
    out_f = jax.ShapeDtypeStruct((m, wa), F32)
    return pl.pallas_call(
        _qkv_kernel,
        out_shape=(out_bf, out_bf, out_bf, out_f, out_f),
        grid=(m // tm, nj),
        in_specs=[
            pl.BlockSpec((tm, d), lambda i, j: (i, 0)),
            pl.BlockSpec((d, tn), lambda i, j: (0, j)),
            pl.BlockSpec((d, tn), lambda i, j: (0, nj + j)),
            pl.BlockSpec((d, tn), lambda i, j: (0, 2 * nj + j)),
        ],
        out_specs=[pl.BlockSpec((tm, tn), row)] * 5,
        scratch_shapes=[pltpu.VMEM((tm, d), BF16)],
        compiler_params=_params("arbitrary", "arbitrary"),
        name="qkv_proj",
    )(x, w, w, w)


def _mlstm_proj_kernel(x_ref, wq_ref, wk_ref, wv_ref, wo_ref, wg_ref, q_ref, k_ref, v_ref, o_ref, g_ref):
    xb = x_ref[...]
    q_ref[...] = _dot(xb, wq_ref[...]).astype(BF16)
    k_ref[...] = _dot(xb, wk_ref[...]).astype(BF16)
    v_ref[...] = _dot(xb, wv_ref[...]).astype(BF16)
    o_ref[...] = _dot(xb, wo_ref[...])

    @pl.when(pl.program_id(1) == 0)
    def _():
        g_ref[...] = _dot(xb, wg_ref[...])


def _mlstm_proj(xb, w, w_gate, qk_w, v_w, tm, nj):
    m, d = xb.shape
    tq = qk_w // nj
    tv = v_w // nj
    return pl.pallas_call(
        _mlstm_proj_kernel,
        out_shape=(
            jax.ShapeDtypeStruct((m, qk_w), BF16),
            jax.ShapeDtypeStruct((m, qk_w), BF16),
            jax.ShapeDtypeStruct((m, v_w), BF16),
            jax.ShapeDtypeStruct((m, v_w), F32),
            jax.ShapeDtypeStruct((m, LANES), F32),
        ),
        grid=(m // tm, nj),
        in_specs=[
            pl.BlockSpec((tm, d), lambda i, j: (i, 0)),
            pl.BlockSpec((d, tq), lambda i, j: (0, j)),
            pl.BlockSpec((d, tq), lambda i, j: (0, nj + j)),
            pl.BlockSpec((d, tv), lambda i, j: (0, (2 * qk_w) // tv + j)),
            pl.BlockSpec((d, tv), lambda i, j: (0, (2 * qk_w + v_w) // tv + j)),
            pl.BlockSpec((d, LANES), lambda i, j: (0, 0)),
        ],
        out_specs=[
            pl.BlockSpec((tm, tq), lambda i, j: (i, j)),
            pl.BlockSpec((tm, tq), lambda i, j: (i, j)),
            pl.BlockSpec((tm, tv), lambda i, j: (i, j)),
            pl.BlockSpec((tm, tv), lambda i, j: (i, j)),
            pl.BlockSpec((tm, LANES), lambda i, j: (i, 0)),
        ],
        compiler_params=_params("arbitrary", "arbitrary"),
        name="mlstm_proj",
    )(xb, w, w, w, w, w_gate)


def _proj_ln_kernel(a_ref, w_ref, r_ref, g_ref, b_ref, of_ref, ob_ref, *, alpha):
    y = _dot(a_ref[...], w_ref[...])
    out = _layer_norm_rows(alpha * r_ref[...] + y, g_ref[...], b_ref[...])
    of_ref[...] = out
    ob_ref[...] = out.astype(BF16)


def _proj_ln(a, w, resid, g, b, alpha, tm):
    m, kd = a.shape
    d = w.shape[1]
    return pl.pallas_call(
        functools.partial(_proj_ln_kernel, alpha=alpha),
        out_shape=(jax.ShapeDtypeStruct((m, d), F32), jax.ShapeDtypeStruct((m, d), BF16)),
        grid=(m // tm,),
        in_specs=[
            pl.BlockSpec((tm, kd), lambda i: (i, 0)),
            pl.BlockSpec((kd, d), lambda i: (0, 0), pipeline_mode=pl.Buffered(1)),
            pl.BlockSpec((tm, d), lambda i: (i, 0)),
            pl.BlockSpec((1, d), lambda i: (0, 0)),
            pl.BlockSpec((1, d), lambda i: (0, 0)),
        ],
        out_specs=[pl.BlockSpec((tm, d), lambda i: (i, 0))] * 2,
        compiler_params=_params("arbitrary"),
        name="proj_ln",
    )(a, w, resid, g, b)


def _silu_gate(cg, cv):
    return (cg * jax.nn.sigmoid(cg) * cv).astype(BF16)


def _ffn_up_prompt_kernel(x_ref, wg_ref, wv_ref, cwg_ref, cwv_ref, cbg_ref, cbv_ref,
                          o_ref, sg_ref, sv_ref, carry_ref, *, tiles_per_seq):
    i = pl.program_id(0)
    j = pl.program_id(1)
    nj = pl.num_programs(1)
    tm = x_ref.shape[0]
    xb = x_ref[...]
    first = (i % tiles_per_seq) == 0

    def conv(u, cw_ref, cb_ref, slot):
        @pl.when(first)
        def _():
            carry_ref[slot] = jnp.zeros((SUBLANES, u.shape[1]), F32)

        prev = carry_ref[slot]
        carry_ref[slot] = u[tm - SUBLANES:, :]
        w0 = cw_ref[0:1, :]
        w1 = cw_ref[1:2, :]
        w2 = cw_ref[2:3, :]
        bias = cb_ref[...]
        c = bias + w0 * pltpu.roll(u, 2, 0) + w1 * pltpu.roll(u, 1, 0) + w2 * u
        rowi = lax.broadcasted_iota(jnp.int32, (SUBLANES, u.shape[1]), 0)
        u8 = u[:SUBLANES, :]
        p0 = prev[SUBLANES - 2:SUBLANES - 1, :]
        p1 = prev[SUBLANES - 1:SUBLANES, :]
        m1 = jnp.where(rowi == 0, p1, pltpu.roll(u8, 1, 0))
        m2 = jnp.where(rowi == 0, p0, jnp.where(rowi == 1, p1, pltpu.roll(u8, 2, 0)))
        c8 = bias + w0 * m2 + w1 * m1 + w2 * u8
        return jnp.concatenate([c8, c[SUBLANES:, :]], axis=0)

    ug = _dot(xb, wg_ref[...])
    uv = _dot(xb, wv_ref[...])
    sg_ref[...] = ug[tm - (CONV_W - 1):, :]
    sv_ref[...] = uv[tm - (CONV_W - 1):, :]
    o_ref[...] = _silu_gate(conv(ug, cwg_ref, cbg_ref, j), conv(uv, cwv_ref, cbv_ref, nj + j))


def _ffn_up_prompt(xb, w_up, conv_w, conv_b, batch, tm, tn):
    m, d = xb.shape
    f = w_up.shape[1] // 2
    nj = f // tn
    seq = m // batch
    tiles_per_seq = seq // tm
    state = jax.ShapeDtypeStruct((m // tm, CONV_W - 1, f), F32)
    g, sg, sv = pl.pallas_call(
        functools.partial(_ffn_up_prompt_kernel, tiles_per_seq=tiles_per_seq),
        out_shape=(jax.ShapeDtypeStruct((m, f), BF16), state, state),
        grid=(m // tm, nj),
        in_specs=[
            pl.BlockSpec((tm, d), lambda i, j: (i, 0)),
            pl.BlockSpec((d, tn), lambda i, j: (0, j)),
            pl.BlockSpec((d, tn), lambda i, j: (0, nj + j)),
            pl.BlockSpec((CONV_W, tn), lambda i, j: (0, j)),
            pl.BlockSpec((CONV_W, tn), lambda i, j: (0, nj + j)),
            pl.BlockSpec((1, tn), lambda i, j: (0, j)),
            pl.BlockSpec((1, tn), lambda i, j: (0, nj + j)),
        ],
        out_specs=[
            pl.BlockSpec((tm, tn), lambda i, j: (i, j)),
            pl.BlockSpec((None, CONV_W - 1, tn), lambda i, j: (i, 0, j)),
            pl.BlockSpec((None, CONV_W - 1, tn), lambda i, j: (i, 0, j)),
        ],
        scratch_shapes=[pltpu.VMEM((2 * nj, SUBLANES, tn), F32)],
        compiler_params=_params("arbitrary", "arbitrary"),
        name="ffn_up_prompt",
    )(xb, w_up, w_up, conv_w, conv_w, conv_b, conv_b)
    last = slice(tiles_per_seq - 1, None, tiles_per_seq)
    return g, jnp.concatenate([sg[last], sv[last]], axis=-1)


def _ffn_up_sample_kernel(x_ref, wg_ref, wv_ref, cwg_ref, cwv_ref, cbg_ref, cbv_ref, pg_ref, pv_ref,
                          o_ref, sg_ref, sv_ref, *, n_t, nb):
    xb = x_ref[...]

    def conv(u, cw_ref, cb_ref, p_ref, s_ref):
        up = [p_ref[0], p_ref[1]] + [u[t * nb:(t + 1) * nb, :] for t in range(n_t)]
        s_ref[0] = up[n_t]
        s_ref[1] = up[n_t + 1]
        bias = cb_ref[...]
        return [bias + cw_ref[0:1, :] * up[t] + cw_ref[1:2, :] * up[t + 1] + cw_ref[2:3, :] * up[t + 2]
                for t in range(n_t)]

    cg = conv(_dot(xb, wg_ref[...]), cwg_ref, cbg_ref, pg_ref, sg_ref)
    cv = conv(_dot(xb, wv_ref[...]), cwv_ref, cbv_ref, pv_ref, sv_ref)
    for t in range(n_t):
        o_ref[t * nb:(t + 1) * nb, :] = _silu_gate(cg[t], cv[t])


def _ffn_up_sample(xb, w_up, conv_w, conv_b, prefix, n_t, tn):
    m, d = xb.shape
    nb = m // n_t
    f = w_up.shape[1] // 2
    nj = f // tn
    state = jax.ShapeDtypeStruct((CONV_W - 1, nb, f), F32)
    g, sg, sv = pl.pallas_call(
        functools.partial(_ffn_up_sample_kernel, n_t=n_t, nb=nb),
        out_shape=(jax.ShapeDtypeStruct((m, f), BF16), state, state),
        grid=(nj,),
        in_specs=[
            pl.BlockSpec((m, d), lambda j: (0, 0)),
            pl.BlockSpec((d, tn), lambda j: (0, j)),
            pl.BlockSpec((d, tn), lambda j: (0, nj + j)),
            pl.BlockSpec((CONV_W, tn), lambda j: (0, j)),
            pl.BlockSpec((CONV_W, tn), lambda j: (0, nj + j)),
            pl.BlockSpec((1, tn), lambda j: (0, j)),
            pl.BlockSpec((1, tn), lambda j: (0, nj + j)),
            pl.BlockSpec((CONV_W - 1, nb, tn), lambda j: (0, 0, j)),
            pl.BlockSpec((CONV_W - 1, nb, tn), lambda j: (0, 0, nj + j)),
        ],
        out_specs=[
            pl.BlockSpec((m, tn), lambda j: (0, j)),
            pl.BlockSpec((CONV_W - 1, nb, tn), lambda j: (0, 0, j)),
            pl.BlockSpec((CONV_W - 1, nb, tn), lambda j: (0, 0, j)),
        ],
        compiler_params=_params("arbitrary"),
        name="ffn_up_sample",
    )(xb, w_up, w_up, conv_w, conv_w, conv_b, conv_b, prefix, prefix)
    return g, jnp.concatenate([sg, sv], axis=-1)


def _sb_prompt_kernel(bias_ref, q_ref, k_ref, v_ref, o_ref, acc_ref, run_ref, *, scale):
    h = pl.program_id(1)
    qi = pl.program_id(2)
    t = q_ref.shape[0]
    bias = bias_ref[h]
    q = q_ref[...]
    row = lax.broadcasted_iota(jnp.int32, (t, t), 0)
    col = lax.broadcasted_iota(jnp.int32, (t, t), 1)
    suffix_incl = (row >= col).astype(BF16)
    acc_ref[...] = jnp.zeros_like(acc_ref)
    run_ref[...] = jnp.zeros_like(run_ref)

    def block(j, diagonal):
        start = pl.multiple_of(j * t, t)
        kj = k_ref[pl.ds(start, t), :]
        vj = v_ref[pl.ds(start, t), :]
        z = _dot_nt(q, kj) * scale + bias
        sp = _softplus(z)
        log_beta = z - sp
        log_keep = -sp
        if diagonal:
            valid = col < row
            log_keep = jnp.where(valid, log_keep, 0.0)
        incl = _split_dot(log_keep, suffix_incl, 2)
        a = jnp.exp(log_beta + (incl - log_keep) + run_ref[...])
        if diagonal:
            a = jnp.where(valid, a, 0.0)
        run_ref[...] += incl[:, 0:1]
        acc_ref[...] += _dot(a.astype(BF16), vj)

    block(qi, True)

    def body(it, carry):
        block(qi - 1 - it, False)
        return carry

    lax.fori_loop(0, qi, body, 0)
    o_ref[...] = acc_ref[...].astype(BF16)


def _sb_prompt(q, k, v, bias, n_heads, t):
    b, s, w = q.shape
    dh = w // n_heads
    return pl.pallas_call(
        functools.partial(_sb_prompt_kernel, scale=dh ** -0.5),
        out_shape=jax.ShapeDtypeStruct((b, s, w), BF16),
        grid=(b, n_heads, s // t),
        in_specs=[
            pl.BlockSpec(memory_space=pltpu.SMEM),
            pl.BlockSpec((None, t, dh), lambda bi, h, qi: (bi, qi, h)),
            pl.BlockSpec((None, s, dh), lambda bi, h, qi: (bi, 0, h)),
            pl.BlockSpec((None, s, dh), lambda bi, h, qi: (bi, 0, h)),
        ],
        out_specs=pl.BlockSpec((None, t, dh), lambda bi, h, qi: (bi, qi, h)),
        scratch_shapes=[pltpu.VMEM((t, dh), F32), pltpu.VMEM((t, 1), F32)],
        compiler_params=_params("arbitrary", "arbitrary", "arbitrary"),
        name="sb_prompt",
    )(bias, q, k, v)


def _sb_sample_kernel(pt_ref, qbd_ref, bias_ref, kn_ref, vn_ref, kp_ref, vp_ref, o_ref, acc_ref, run_ref,
                      *, scale, n_heads, n_new):
    p = pl.program_id(1)
    n_pages = pl.num_programs(1)
    page = kp_ref.shape[0]
    dh = kp_ref.shape[1] // n_heads
    qbd = qbd_ref[...]
    bias = bias_ref[...]
    ncol = qbd.shape[0]
    row = lax.broadcasted_iota(jnp.int32, (page, page), 0)
    col = lax.broadcasted_iota(jnp.int32, (page, page), 1)
    suffix_incl = (row >= col).astype(BF16)

    def block(keys, vals, valid):
        z = _dot_nt(qbd, keys) * scale + bias
        sp = _softplus(z)
        log_beta = z - sp
        log_keep = -sp
        if valid is not None:
            log_keep = jnp.where(valid, log_keep, 0.0)
        incl = _split_dot(log_keep, suffix_incl, 2)
        a = jnp.exp(log_beta + (incl - log_keep) + run_ref[...])
        if valid is not None:
            a = jnp.where(valid, a, 0.0)
        run_ref[...] += incl[:, 0:1]
        acc_ref[...] += _dot(a.astype(BF16), vals)

    @pl.when(p == 0)
    def _():
        acc_ref[...] = jnp.zeros_like(acc_ref)
        run_ref[...] = jnp.zeros_like(run_ref)
        qry = lax.broadcasted_iota(jnp.int32, (ncol, page), 0) % SUBLANES
        key = lax.broadcasted_iota(jnp.int32, (ncol, page), 1)
        block(kn_ref[...], vn_ref[...], (key < qry) & (qry < n_new))

    block(kp_ref[...].astype(BF16), vp_ref[...].astype(BF16), None)

    @pl.when(p == n_pages - 1)
    def _():
        for h in range(n_heads):
            o_ref[:, h * dh:(h + 1) * dh] = acc_ref[h * SUBLANES:(h + 1) * SUBLANES, h * dh:(h + 1) * dh]


def _sb_sample(page_table, qbd, bias_cols, k_new, v_new, cache_k, cache_v, n_heads, n_new):
    nb, n_pages = page_table.shape
    _, page, w = cache_k.shape
    ncol = qbd.shape[1]
    dh = w // n_heads
    last = n_pages - 1
    grid_spec = pltpu.PrefetchScalarGridSpec(
        num_scalar_prefetch=1,
        grid=(nb, n_pages),
        in_specs=[
            pl.BlockSpec((None, ncol, w), lambda b, p, pt: (b, 0, 0)),
            pl.BlockSpec((ncol, 1), lambda b, p, pt: (0, 0)),
            pl.BlockSpec((None, page, w), lambda b, p, pt: (b, 0, 0)),
            pl.BlockSpec((None, page, w), lambda b, p, pt: (b, 0, 0)),
            pl.BlockSpec((None, page, w), lambda b, p, pt: (pt[b, last - p], 0, 0)),
            pl.BlockSpec((None, page, w), lambda b, p, pt: (pt[b, last - p], 0, 0)),
        ],
        out_specs=pl.BlockSpec((None, SUBLANES, w), lambda b, p, pt: (b, 0, 0)),
        scratch_shapes=[pltpu.VMEM((ncol, w), F32), pltpu.VMEM((ncol, 1), F32)],
    )
    return pl.pallas_call(
        functools.partial(_sb_sample_kernel, scale=dh ** -0.5, n_heads=n_heads, n_new=n_new),
        out_shape=jax.ShapeDtypeStruct((nb, SUBLANES, w), F32),
        grid_spec=grid_spec,
        compiler_params=_params("arbitrary", "arbitrary"),
        name="sb_sample",
    )(page_table, qbd, bias_cols, k_new, v_new, cache_k, cache_v)


def _mlstm_kernel(*refs, n_heads, n_valid, has_init):
    if has_init:
        (q_ref, k_ref, v_ref, o_ref, g_ref, bg_ref, gh_ref, c0_ref, n0_ref, m0_ref,
         y_ref, c_ref, n_ref, m_ref) = refs
    else:
        q_ref, k_ref, v_ref, o_ref, g_ref, bg_ref, gh_ref, y_ref, c_ref, n_ref, m_ref = refs
    chunk = pl.program_id(1)
    l = q_ref.shape[0]
    dk = q_ref.shape[1] // n_heads
    dv = v_ref.shape[1] // n_heads

    @pl.when(chunk == 0)
    def _():
        if has_init:
            c_ref[...] = c0_ref[...]
            n_ref[...] = n0_ref[...]
            m_ref[...] = m0_ref[...]
        else:
            c_ref[...] = jnp.zeros_like(c_ref)
            n_ref[...] = jnp.zeros_like(n_ref)
            m_ref[...] = jnp.zeros_like(m_ref)

    gates = g_ref[...] + bg_ref[...]
    lane = lax.broadcasted_iota(jnp.int32, gates.shape, 1)
    tok = lax.broadcasted_iota(jnp.int32, gates.shape, 0)
    is_forget = (lane >= n_heads) & (lane < 2 * n_heads)
    log_f = jnp.where(is_forget, -_softplus(-gates), 0.0)
    if n_valid < l:
        log_f = jnp.where(tok < n_valid, log_f, 0.0)
        gates = jnp.where(tok < n_valid, gates, -jnp.inf)
    ti = lax.broadcasted_iota(jnp.int32, (l, l), 0)
    si = lax.broadcasted_iota(jnp.int32, (l, l), 1)
    causal = si <= ti
    diag = si == ti
    tri = causal.astype(BF16)
    cum_col = _split_dot(log_f, tri, 3, left=True)

    def to_row(column):
        return jnp.sum(jnp.where(diag, column, 0.0), axis=0, keepdims=True)

    for h in range(n_heads):
        q = q_ref[:, h * dk:(h + 1) * dk]
        k = k_ref[:, h * dk:(h + 1) * dk]
        v = v_ref[:, h * dv:(h + 1) * dv]
        c = c_ref[h]
        n = n_ref[h:h + 1, :]
        m = m_ref[h:h + 1, 0:1]
        b_col = cum_col[:, n_heads + h:n_heads + h + 1]
        ig_col = gates[:, h:h + 1]
        b_row = to_row(b_col)
        ig_row = to_row(ig_col)

        d = jnp.where(causal, b_col - b_row + ig_row, -jnp.inf)
        inter = b_col + m
        m_t = jnp.maximum(inter, jnp.max(d, axis=1, keepdims=True))
        dw = jnp.exp(d - m_t)
        wi = jnp.exp(inter - m_t)
        s = _dot_nt(q, k) * (dw * dk ** -0.5)
        num = wi * _dot_nt(q, c.astype(BF16)) + _dot(s.astype(BF16), v)
        qn = jnp.sum(q.astype(F32) * n.astype(BF16).astype(F32), axis=1, keepdims=True)
        den = wi * qn + jnp.sum(s, axis=1, keepdims=True)
        hid = num / jnp.maximum(jnp.abs(den), jnp.exp(-m_t))

        b_last = b_col[l - 1:l, :]
        g_col = b_last - b_col + ig_col
        m_new = jnp.maximum(b_last + m, jnp.max(g_col, axis=0, keepdims=True))
        w_c = jnp.exp(b_last + m - m_new)
        w_s = jnp.exp(g_col - m_new)
        kw = w_s * (k.astype(F32) * dk ** -0.5)
        c_ref[h] = w_c * c + _dot_tn(v, kw.astype(BF16))
        n_ref[h:h + 1, :] = w_c * n + jnp.sum(kw, axis=0, keepdims=True)
        m_ref[h:h + 1, :] = jnp.broadcast_to(m_new, (1, m_ref.shape[1]))

        hid = hid * lax.rsqrt(jnp.mean(hid * hid, axis=1, keepdims=True) + HEAD_EPS)
        gate = jax.nn.sigmoid(o_ref[:, h * dv:(h + 1) * dv])
        y_ref[:, h * dv:(h + 1) * dv] = (hid * gh_ref[:, h * dv:(h + 1) * dv] * gate).astype(BF16)


def _mlstm(q, k, v, o, gates, b_gate, g_head, n_heads, l, n_valid, init=None):
    nb, t, qk_w = q.shape
    v_w = v.shape[2]
    dk = qk_w // n_heads
    dv = v_w // n_heads
    tok = lambda width: pl.BlockSpec((None, l, width), lambda b, c: (b, c, 0))
    const = lambda shape: pl.BlockSpec(shape, lambda b, c: (0,) * len(shape))
    c_spec = pl.BlockSpec((None, n_heads, dv, dk), lambda b, c: (b, 0, 0, 0))
    n_spec = pl.BlockSpec((None, n_heads, dk), lambda b, c: (b, 0, 0))
    m_spec = pl.BlockSpec((None, n_heads, LANES), lambda b, c: (b, 0, 0))
    in_specs = [tok(qk_w), tok(qk_w), tok(v_w), tok(v_w), tok(LANES), const((1, LANES)), const((1, v_w))]
    args = [q, k, v, o, gates, b_gate, g_head]
    if init is not None:
        in_specs += [c_spec, n_spec, m_spec]
        args += list(init)
    return pl.pallas_call(
        functools.partial(_mlstm_kernel, n_heads=n_heads, n_valid=n_valid, has_init=init is not None),
        out_shape=(
            jax.ShapeDtypeStruct((nb, t, v_w), BF16),
            jax.ShapeDtypeStruct((nb, n_heads, dv, dk), F32),
            jax.ShapeDtypeStruct((nb, n_heads, dk), F32),
            jax.ShapeDtypeStruct((nb, n_heads, LANES), F32),
        ),
        grid=(nb, t // l),
        in_specs=in_specs,
        out_specs=[tok(v_w), c_spec, n_spec, m_spec],
        compiler_params=_params("arbitrary", "arbitrary"),
        name="mlstm",
    )(*args)


def _pick(n, candidates):
    for c in candidates:
        if n % c == 0:
            return c
    return n


def kernel(x_prompt, x_sample, cache_k, cache_v, page_table, state_C, state_n, state_m, state_conv, w_in_a, w_out_a, sb_bias, w_in_b, b_gate_b, g_head_b, w_out_b, w_up, conv_w, conv_b, w_down, ln1_g, ln1_b, ln2_g, ln2_b):
    depth = w_up.shape[0]
    alpha = (2 * depth) ** 0.25
    bp, seq, d_model = x_prompt.shape
    bs, t_new, _ = x_sample.shape
    n_heads_a = sb_bias.shape[1]
    w_a = w_in_a.shape[2] // 3
    dh_a = w_a // n_heads_a
    n_heads_b = state_C.shape[2]
    dv_b, dk_b = state_C.shape[3], state_C.shape[4]
    qk_w, v_w = n_heads_b * dk_b, n_heads_b * dv_b
    d_ff = w_up.shape[2] // 2
    page = cache_k.shape[2]
    assert bs == SUBLANES and t_new <= SUBLANES and w_a == d_model

    mp = bp * seq
    ms = bs * t_new
    tm_p = _pick(seq, TM_PROMPT)
    tn_f = _pick(d_ff, TN_FF)

    xp = x_prompt.reshape(mp, d_model)
    xs = x_sample.transpose(1, 0, 2).reshape(ms, d_model)

    def to_batch_major(a, pad_to=None):
        a = a.reshape(t_new, bs, -1).transpose(1, 0, 2)
        if pad_to is not None:
            a = jnp.pad(a, ((0, 0), (0, pad_to - t_new), (0, 0)))
        return a

    def to_time_major(a):
        return a[:, :t_new].transpose(1, 0, 2).reshape(ms, -1)

    outs_p = dict(k=[], v=[], c=[], n=[], m=[], conv=[])
    outs_s = dict(k=[], v=[], c=[], n=[], m=[], conv=[])
    xp_f, xs_f = xp, xs
    xp_b = xs_b = None

    for i in range(depth):
        j = i // N_MIXERS
        if i % N_MIXERS == 0:
            w_in = w_in_a[j].astype(BF16)
            w_out = w_out_a[j].astype(BF16)
            qb, kb, vb, kf, vf = _qkv_proj(xp_f, w_in, tm_p, TN_QKV)
            outs_p["k"].append(kf.reshape(bp, seq, n_heads_a, dh_a))
            outs_p["v"].append(vf.reshape(bp, seq, n_heads_a, dh_a))
            shp = (bp, seq, w_a)
            mix_p = _sb_prompt(qb.reshape(shp), kb.reshape(shp), vb.reshape(shp), sb_bias[j], n_heads_a,
                               _pick(seq, T_ATTN)).reshape(mp, w_a)
            qb, kb, vb, kf, vf = _qkv_proj(xs_f, w_in, ms, TN_QKV)
            outs_s["k"].append(to_batch_major(kf).reshape(bs, t_new, n_heads_a, dh_a))
            outs_s["v"].append(to_batch_major(vf).reshape(bs, t_new, n_heads_a, dh_a))
            q8 = to_batch_major(qb, SUBLANES).reshape(bs, SUBLANES, n_heads_a, dh_a)
            eye = jnp.eye(n_heads_a, dtype=BF16)
            qbd = (q8[:, None, :, :, :] * eye[None, :, None, :, None]).reshape(bs, n_heads_a * SUBLANES, w_a)
            bias_cols = jnp.repeat(sb_bias[j], SUBLANES)[:, None]
            pool = cache_k.shape[1]
            o_s = _sb_sample(page_table, qbd, bias_cols, to_batch_major(kb, page), to_batch_major(vb, page),
                             cache_k[j].reshape(pool, page, w_a), cache_v[j].reshape(pool, page, w_a),
                             n_heads_a, t_new)
            mix_s = to_time_major(o_s).astype(BF16)
        else:
            w_in = w_in_b[j].astype(BF16)
            w_out = w_out_b[j].astype(BF16)
            n_main = 2 * qk_w + 2 * v_w
            w_gate = jnp.pad(w_in[:, n_main:], ((0, 0), (0, LANES - 2 * n_heads_b)))
            b_gate = jnp.pad(b_gate_b[j], (0, LANES - 2 * n_heads_b))[None, :]
            g_head = g_head_b[j][None, :]
            q, k, v, o, g = _mlstm_proj(xp_b, w_in, w_gate, qk_w, v_w, tm_p, 4)
            r3 = lambda a: a.reshape(bp, seq, -1)
            l = MLSTM_CHUNK if seq % MLSTM_CHUNK == 0 else seq
            mix_p, c, n, m = _mlstm(r3(q), r3(k), r3(v), r3(o), r3(g), b_gate, g_head, n_heads_b, l, l)
            mix_p = mix_p.reshape(mp, v_w)
            outs_p["c"].append(c)
            outs_p["n"].append(n)
            outs_p["m"].append(m[:, :, 0])
            q, k, v, o, g = _mlstm_proj(xs_b, w_in, w_gate, qk_w, v_w, ms, 4)
            ls = MLSTM_CHUNK if t_new % MLSTM_CHUNK == 0 else t_new
            assert ls == t_new and t_new <= MLSTM_CHUNK
            padl = lambda a: to_batch_major(a, MLSTM_CHUNK)
            m0 = jnp.broadcast_to(state_m[j][:, :, None], (bs, n_heads_b, LANES))
            y, c, n, m = _mlstm(padl(q), padl(k), padl(v), padl(o), padl(g), b_gate, g_head, n_heads_b,
                                MLSTM_CHUNK, t_new, init=(state_C[j], state_n[j], m0))
            mix_s = to_time_major(y)
            outs_s["c"].append(c)
            outs_s["n"].append(n)
            outs_s["m"].append(m[:, :, 0])

        g1, b1 = ln1_g[i][None, :], ln1_b[i][None, :]
        g2, b2 = ln2_g[i][None, :], ln2_b[i][None, :]
        w_u = w_up[i].astype(BF16)
        w_d = w_down[i].astype(BF16)
        cw, cb = conv_w[i], conv_b[i][None, :]

        xp_f, xp_b = _proj_ln(mix_p, w_out, xp_f, g1, b1, alpha, _pick(seq, TM_LN))
        gp, conv_p = _ffn_up_prompt(xp_b, w_u, cw, cb, bp, tm_p, tn_f)
        outs_p["conv"].append(conv_p)
        xp_f, xp_b = _proj_ln(gp, w_d, xp_f, g2, b2, alpha, _pick(seq, TM_DOWN))

        xs_f, xs_b = _proj_ln(mix_s, w_out, xs_f, g1, b1, alpha, ms)
        gs, conv_s = _ffn_up_sample(xs_b, w_u, cw, cb, state_conv[i].transpose(1, 0, 2), t_new, tn_f)
        outs_s["conv"].append(conv_s.transpose(1, 0, 2))
        xs_f, xs_b = _proj_ln(gs, w_d, xs_f, g2, b2, alpha, ms)

    y_prompt = xp_f.reshape(bp, seq, d_model)
    y_sample = xs_f.reshape(t_new, bs, d_model).transpose(1, 0, 2)
    st = jnp.stack
    return (y_prompt, y_sample, st(outs_p["k"]), st(outs_p["v"]), st(outs_s["k"]), st(outs_s["v"]),
            st(outs_p["c"]), st(outs_p["n"]), st(outs_p["m"]), st(outs_s["c"]), st(outs_s["n"]), st(outs_s["m"]),
            st(outs_p["conv"]), st(outs_s["conv"]))
```

```python
import functools

import jax
import jax.numpy as jnp
from jax import lax
from jax.experimental import pallas as pl
from jax.experimental.pallas import tpu as pltpu

F32 = jnp.float32
BF16 = jnp.bfloat16

LN_EPS = 1e-5
HEAD_EPS = 1e-6
CONV_W = 3
MLSTM_CHUNK = 128
N_MIXERS = 2

SUBLANES = 8
LANES = 128
MXU_DIM = 256
VMEM_LIMIT_BYTES = 56 * 1024 * 1024

TM_PROMPT = (1024, 512, 256, 128)
TM_LN = (512, 256, 128)
TM_DOWN = (256, 128)
TN_FF = (512, 256, 128)
TN_QKV = 256
T_ATTN = (MXU_DIM, LANES)
ATTN_SUBTILES = (4, 2, 1)
PAGES_PER_STEP = (4, 2, 1)


def _params(*semantics):
    return pltpu.CompilerParams(dimension_semantics=semantics, vmem_limit_bytes=VMEM_LIMIT_BYTES)


def _dot(a, b):
    return jnp.dot(a, b, preferred_element_type=F32)


def _dot_nt(a, b):
    return lax.dot_general(a, b, (((1,), (1,)), ((), ())), preferred_element_type=F32)


def _dot_tn(a, b):
    return lax.dot_general(a, b, (((0,), (0,)), ((), ())), preferred_element_type=F32)


def _split_dot(x, t, n_terms, left=False):
    acc = None
    rem = x
    for i in range(n_terms):
        part = rem.astype(BF16)
        term = _dot(t, part) if left else _dot(part, t)
        acc = term if acc is None else acc + term
        if i + 1 < n_terms:
            rem = rem - part.astype(F32)
    return acc


def _softplus(z):
    return jnp.maximum(z, 0.0) + jnp.log(1.0 + jnp.exp(-jnp.abs(z)))


def _softplus_log1p(z):
    return jnp.maximum(z, 0.0) + jnp.log1p(jnp.exp(-jnp.abs(z)))


def _layer_norm_rows(z, g, b):
    mu = jnp.mean(z, axis=-1, keepdims=True)
    zc = z - mu
    var = jnp.mean(zc * zc, axis=-1, keepdims=True)
    return zc * lax.rsqrt(var + LN_EPS) * g + b


def _qkv_kernel(x_ref, wq_ref, wk_ref, wv_ref, qb_ref, kb_ref, vb_ref, kf_ref, vf_ref, xb_ref, *, q_scale):
    @pl.when(pl.program_id(1) == 0)
    def _():
        xb_ref[...] = x_ref[...].astype(BF16)

    xb = xb_ref[...]
    qb_ref[...] = (_dot(xb, wq_ref[...]) * q_scale).astype(BF16)
    k = _dot(xb, wk_ref[...])
    kf_ref[...] = k
    kb_ref[...] = k.astype(BF16)
    v = _dot(xb, wv_ref[...])
    vf_ref[...] = v
    vb_ref[...] = v.astype(BF16)


def _qkv_proj(x, w, q_scale, tm, tn):
    m, d = x.shape
    wa = w.shape[1] // 3
    nj = wa // tn
    row = lambda i, j: (i, j)
    out_bf = jax.ShapeDtypeStruct((m, wa), BF16)
    out_f = jax.ShapeDtypeStruct((m, wa), F32)
    return pl.pallas_call(
        functools.partial(_qkv_kernel, q_scale=q_scale),
        out_shape=(out_bf, out_bf, out_bf, out_f, out_f),
        grid=(m // tm, nj),
        in_specs=[
            pl.BlockSpec((tm, d), lambda i, j: (i, 0)),
            pl.BlockSpec((d, tn), lambda i, j: (0, j)),
            pl.BlockSpec((d, tn), lambda i, j: (0, nj + j)),
            pl.BlockSpec((d, tn), lambda i, j: (0, 2 * nj + j)),
        ],
        out_specs=[pl.BlockSpec((tm, tn), row)] * 5,
        scratch_shapes=[pltpu.VMEM((tm, d), BF16)],
        compiler_params=_params("arbitrary", "arbitrary"),
        name="qkv_proj",
    )(x, w, w, w)


def _mlstm_proj_kernel(x_ref, wq_ref, wk_ref, wv_ref, wo_ref, wg_ref, q_ref, k_ref, v_ref, o_ref, g_ref):
    xb = x_ref[...]
    q_ref[...] = _dot(xb, wq_ref[...]).astype(BF16)
    k_ref[...] = _dot(xb, wk_ref[...]).astype(BF16)
    v_ref[...] = _dot(xb, wv_ref[...]).astype(BF16)
    o_ref[...] = _dot(xb, wo_ref[...])

    @pl.when(pl.program_id(1) == 0)
    def _():
        g_ref[...] = _dot(xb, wg_ref[...])


def _mlstm_proj(xb, w, w_gate, qk_w, v_w, tm, nj):
    m, d = xb.shape
    tq = qk_w // nj
    tv = v_w // nj
    return pl.pallas_call(
        _mlstm_proj_kernel,
        out_shape=(
            jax.ShapeDtypeStruct((m, qk_w), BF16),
            jax.ShapeDtypeStruct((m, qk_w), BF16),
            jax.ShapeDtypeStruct((m, v_w), BF16),
            jax.ShapeDtypeStruct((m, v_w), F32),
            jax.ShapeDtypeStruct((m, LANES), F32),
        ),
        grid=(m // tm, nj),
        in_specs=[
            pl.BlockSpec((tm, d), lambda i, j: (i, 0)),
            pl.BlockSpec((d, tq), lambda i, j: (0, j)),
            pl.BlockSpec((d, tq), lambda i, j: (0, nj + j)),
            pl.BlockSpec((d, tv), lambda i, j: (0, (2 * qk_w) // tv + j)),
            pl.BlockSpec((d, tv), lambda i, j: (0, (2 * qk_w + v_w) // tv + j)),
            pl.BlockSpec((d, LANES), lambda i, j: (0, 0)),
        ],
        out_specs=[
            pl.BlockSpec((tm, tq), lambda i, j: (i, j)),
            pl.BlockSpec((tm, tq), lambda i, j: (i, j)),
            pl.BlockSpec((tm, tv), lambda i, j: (i, j)),
            pl.BlockSpec((tm, tv), lambda i, j: (i, j)),
            pl.BlockSpec((tm, LANES), lambda i, j: (i, 0)),
        ],
        compiler_params=_params("arbitrary", "arbitrary"),
        name="mlstm_proj",
    )(xb, w, w, w, w, w_gate)


def _proj_ln_kernel(a_ref, w_ref, r_ref, g_ref, b_ref, of_ref, ob_ref, *, alpha):
    y = _dot(a_ref[...], w_ref[...])
    out = _layer_norm_rows(alpha * r_ref[...] + y, g_ref[...], b_ref[...])
    of_ref[...] = out
    ob_ref[...] = out.astype(BF16)


def _proj_ln(a, w, resid, g, b, alpha, tm):
    m, kd = a.shape
    d = w.shape[1]
    return pl.pallas_call(
        functools.partial(_proj_ln_kernel, alpha=alpha),
        out_shape=(jax.ShapeDtypeStruct((m, d), F32), jax.ShapeDtypeStruct((m, d), BF16)),
        grid=(m // tm,),
        in_specs=[
            pl.BlockSpec((tm, kd), lambda i: (i, 0)),
            pl.BlockSpec((kd, d), lambda i: (0, 0), pipeline_mode=pl.Buffered(1)),
            pl.BlockSpec((tm, d), lambda i: (i, 0)),
            pl.BlockSpec((1, d), lambda i: (0, 0)),
            pl.BlockSpec((1, d), lambda i: (0, 0)),
        ],
        out_specs=[pl.BlockSpec((tm, d), lambda i: (i, 0))] * 2,
        compiler_params=_params("arbitrary"),
        name="proj_ln",
    )(a, w, resid, g, b)


def _silu_gate(cg, cv):
    return (cg * jax.nn.sigmoid(cg) * cv).astype(BF16)


def _ffn_up_prompt_kernel(x_ref, wg_ref, wv_ref, cwg_ref, cwv_ref, cbg_ref, cbv_ref,
                          o_ref, sg_ref, sv_ref, carry_ref, *, tiles_per_seq):
    i = pl.program_id(0)
    j = pl.program_id(1)
    nj = pl.num_programs(1)
    tm = x_ref.shape[0]
    xb = x_ref[...]
    first = (i % tiles_per_seq) == 0

    def conv(u, cw_ref, cb_ref, slot):
        @pl.when(first)
        def _():
            carry_ref[slot] = jnp.zeros((SUBLANES, u.shape[1]), F32)

        prev = carry_ref[slot]
        carry_ref[slot] = u[tm - SUBLANES:, :]
        w0 = cw_ref[0:1, :]
        w1 = cw_ref[1:2, :]
        w2 = cw_ref[2:3, :]
        bias = cb_ref[...]
        c = bias + w0 * pltpu.roll(u, 2, 0) + w1 * pltpu.roll(u, 1, 0) + w2 * u
        rowi = lax.broadcasted_iota(jnp.int32, (SUBLANES, u.shape[1]), 0)
        u8 = u[:SUBLANES, :]
        p0 = prev[SUBLANES - 2:SUBLANES - 1, :]
        p1 = prev[SUBLANES - 1:SUBLANES, :]
        m1 = jnp.where(rowi == 0, p1, pltpu.roll(u8, 1, 0))
        m2 = jnp.where(rowi == 0, p0, jnp.where(rowi == 1, p1, pltpu.roll(u8, 2, 0)))
        c8 = bias + w0 * m2 + w1 * m1 + w2 * u8
        return jnp.concatenate([c8, c[SUBLANES:, :]], axis=0)

    ug = _dot(xb, wg_ref[...])
    uv = _dot(xb, wv_ref[...])
    sg_ref[...] = ug[tm - (CONV_W - 1):, :]
    sv_ref[...] = uv[tm - (CONV_W - 1):, :]
    o_ref[...] = _silu_gate(conv(ug, cwg_ref, cbg_ref, j), conv(uv, cwv_ref, cbv_ref, nj + j))


def _ffn_up_prompt(xb, w_up, conv_w, conv_b, batch, tm, tn):
    m, d = xb.shape
    f = w_up.shape[1] // 2
    nj = f // tn
    seq = m // batch
    tiles_per_seq = seq // tm
    state = jax.ShapeDtypeStruct((m // tm, CONV_W - 1, f), F32)
    g, sg, sv = pl.pallas_call(
        functools.partial(_ffn_up_prompt_kernel, tiles_per_seq=tiles_per_seq),
        out_shape=(jax.ShapeDtypeStruct((m, f), BF16), state, state),
        grid=(m // tm, nj),
        in_specs=[
            pl.BlockSpec((tm, d), lambda i, j: (i, 0)),
            pl.BlockSpec((d, tn), lambda i, j: (0, j)),
            pl.BlockSpec((d, tn), lambda i, j: (0, nj + j)),
            pl.BlockSpec((CONV_W, tn), lambda i, j: (0, j)),
            pl.BlockSpec((CONV_W, tn), lambda i, j: (0, nj + j)),
            pl.BlockSpec((1, tn), lambda i, j: (0, j)),
            pl.BlockSpec((1, tn), lambda i, j: (0, nj + j)),
        ],
        out_specs=[
            pl.BlockSpec((tm, tn), lambda i, j: (i, j)),
            pl.BlockSpec((None, CONV_W - 1, tn), lambda i, j: (i, 0, j)),
            pl.BlockSpec((None, CONV_W - 1, tn), lambda i, j: (i, 0, j)),
        ],
        scratch_shapes=[pltpu.VMEM((2 * nj, SUBLANES, tn), F32)],
        compiler_params=_params("arbitrary", "arbitrary"),
        name="ffn_up_prompt",
    )(xb, w_up, w_up, conv_w, conv_w, conv_b, conv_b)
    last = slice(tiles_per_seq - 1, None, tiles_per_seq)
    return g, jnp.concatenate([sg[last], sv[last]], axis=-1)


def _ffn_up_sample_kernel(x_ref, wg_ref, wv_ref, cwg_ref, cwv_ref, cbg_ref, cbv_ref, pg_ref, pv_ref,
                          o_ref, sg_ref, sv_ref, *, n_t, nb):
    xb = x_ref[...]

    def conv(u, cw_ref, cb_ref, p_ref, s_ref):
        up = [p_ref[0], p_ref[1]] + [u[t * nb:(t + 1) * nb, :] for t in range(n_t)]
        s_ref[0] = up[n_t]
        s_ref[1] = up[n_t + 1]
        bias = cb_ref[...]
        return [bias + cw_ref[0:1, :] * up[t] + cw_ref[1:2, :] * up[t + 1] + cw_ref[2:3, :] * up[t + 2]
                for t in range(n_t)]

    cg = conv(_dot(xb, wg_ref[...]), cwg_ref, cbg_ref, pg_ref, sg_ref)
    cv = conv(_dot(xb, wv_ref[...]), cwv_ref, cbv_ref, pv_ref, sv_ref)
    for t in range(n_t):
        o_ref[t * nb:(t + 1) * nb, :] = _silu_gate(cg[t], cv[t])


def _ffn_up_sample(xb, w_up, conv_w, conv_b, prefix, n_t, tn):
    m, d = xb.shape
    nb = m // n_t
    f = w_up.shape[1] // 2
    nj = f // tn
    state = jax.ShapeDtypeStruct((CONV_W - 1, nb, f), F32)
    g, sg, sv = pl.pallas_call(
        functools.partial(_ffn_up_sample_kernel, n_t=n_t, nb=nb),
        out_shape=(jax.ShapeDtypeStruct((m, f), BF16), state, state),
        grid=(nj,),
        in_specs=[
            pl.BlockSpec((m, d), lambda j: (0, 0)),
            pl.BlockSpec((d, tn), lambda j: (0, j)),
            pl.BlockSpec((d, tn), lambda j: (0, nj + j)),
            pl.BlockSpec((CONV_W, tn), lambda j: (0, j)),
            pl.BlockSpec((CONV_W, tn), lambda j: (0, nj + j)),
            pl.BlockSpec((1, tn), lambda j: (0, j)),
            pl.BlockSpec((1, tn), lambda j: (0, nj + j)),
            pl.BlockSpec((CONV_W - 1, nb, tn), lambda j: (0, 0, j)),
            pl.BlockSpec((CONV_W - 1, nb, tn), lambda j: (0, 0, nj + j)),
        ],
        out_specs=[
            pl.BlockSpec((m, tn), lambda j: (0, j)),
            pl.BlockSpec((CONV_W - 1, nb, tn), lambda j: (0, 0, j)),
            pl.BlockSpec((CONV_W - 1, nb, tn), lambda j: (0, 0, j)),
        ],
        compiler_params=_params("arbitrary"),
        name="ffn_up_sample",
    )(xb, w_up, w_up, conv_w, conv_w, conv_b, conv_b, prefix, prefix)
    return g, jnp.concatenate([sg, sv], axis=-1)


def _sb_prompt_kernel(bias_ref, q_ref, k_ref, v_ref, o_ref, acc_ref, run_ref, *, t, n_sub):
    h = pl.program_id(1)
    qi = pl.program_id(2)
    bias = bias_ref[h]
    row = lax.broadcasted_iota(jnp.int32, (t, t), 0)
    col = lax.broadcasted_iota(jnp.int32, (t, t), 1)
    suffix_incl = (row >= col).astype(BF16)
    acc_ref[...] = jnp.zeros_like(acc_ref)
    run_ref[...] = jnp.zeros_like(run_ref)

    def block(j, subs):
        start = pl.multiple_of(j * t, t)
        kj = k_ref[pl.ds(start, t), :]
        vj = v_ref[pl.ds(start, t), :]
        valid = col < row
        zs = [_dot_nt(q_ref[s * t:(s + 1) * t, :], kj) + bias for s, _ in subs]
        sps = [_softplus(z) for z in zs]
        log_betas = [z - sp for z, sp in zip(zs, sps)]
        sps = [jnp.where(valid, sp, 0.0) if diagonal else sp for sp, (_, diagonal) in zip(sps, subs)]
        his = [sp.astype(BF16) for sp in sps]
        los = [(sp - hi.astype(F32)).astype(BF16) for sp, hi in zip(sps, his)]
        incls = [_dot(hi, suffix_incl) + _dot(lo, suffix_incl) for hi, lo in zip(his, los)]
        probs = []
        for (s, diagonal), log_beta, sp, incl in zip(subs, log_betas, sps, incls):
            a = jnp.exp(log_beta - (incl - sp) - run_ref[s])
            probs.append((jnp.where(valid, a, 0.0) if diagonal else a).astype(BF16))
            run_ref[s] += incl[:, 0:1]
        outs = [_dot(a, vj) for a in probs]
        for (s, _), out in zip(subs, outs):
            acc_ref[s] += out

    for r in range(n_sub - 1, -1, -1):
        block(n_sub * qi + r, [(s, s == r) for s in range(r, n_sub)])

    def body(it, carry):
        block(n_sub * qi - 1 - it, [(s, False) for s in range(n_sub)])
        return carry

    lax.fori_loop(0, n_sub * qi, body, 0)
    for s in range(n_sub):
        o_ref[s * t:(s + 1) * t, :] = acc_ref[s].astype(BF16)


def _sb_prompt(q, k, v, bias, n_heads, t, n_sub):
    b, s, w = q.shape
    dh = w // n_heads
    tq = t * n_sub
    return pl.pallas_call(
        functools.partial(_sb_prompt_kernel, t=t, n_sub=n_sub),
        out_shape=jax.ShapeDtypeStruct((b, s, w), BF16),
        grid=(b, n_heads, s // tq),
        in_specs=[
            pl.BlockSpec(memory_space=pltpu.SMEM),
            pl.BlockSpec((None, tq, dh), lambda bi, h, qi: (bi, qi, h)),
            pl.BlockSpec((None, s, dh), lambda bi, h, qi: (bi, 0, h)),
            pl.BlockSpec((None, s, dh), lambda bi, h, qi: (bi, 0, h)),
        ],
        out_specs=pl.BlockSpec((None, tq, dh), lambda bi, h, qi: (bi, qi, h)),
        scratch_shapes=[pltpu.VMEM((n_sub, t, dh), F32), pltpu.VMEM((n_sub, t, 1), F32)],
        compiler_params=_params("arbitrary", "arbitrary", "arbitrary"),
        name="sb_prompt",
    )(bias, q, k, v)


def _sb_sample_kernel(pt_ref, q_ref, bias_ref, kn_ref, vn_ref, *refs, n_heads, n_new, pages_per_step):
    kp_refs = refs[:pages_per_step]
    vp_refs = refs[pages_per_step:2 * pages_per_step]
    o_ref, acc_ref, run_ref = refs[2 * pages_per_step:]
    p = pl.program_id(1)
    page = kp_refs[0].shape[0] // n_heads
    q = q_ref[...]
    bias = bias_ref[...]
    ncol = q.shape[0]
    row = lax.broadcasted_iota(jnp.int32, (page, page), 0)
    col = lax.broadcasted_iota(jnp.int32, (page, page), 1)
    suffix_incl = (row >= col).astype(BF16)

    def head_rows(a, h):
        return a[h * SUBLANES:(h + 1) * SUBLANES]

    def block(keys, vals, valid):
        zt = _dot_nt(q, keys)
        z = jnp.concatenate([head_rows(zt, h)[:, h * page:(h + 1) * page] for h in range(n_heads)], axis=0)
        z = z + bias
        sp = _softplus(z)
        log_beta = z - sp
        if valid is not None:
            sp = jnp.where(valid, sp, 0.0)
        incl = _split_dot(sp, suffix_incl, 2)
        a = jnp.exp(log_beta - (incl - sp) - run_ref[...])
        if valid is not None:
            a = jnp.where(valid, a, 0.0)
        run_ref[...] += incl[:, 0:1]
        blocks = []
        for h in range(n_heads):
            parts = []
            if h > 0:
                parts.append(jnp.zeros((h * SUBLANES, page), F32))
            parts.append(head_rows(a, h))
            if h + 1 < n_heads:
                parts.append(jnp.zeros((ncol - (h + 1) * SUBLANES, page), F32))
            blocks.append(jnp.concatenate(parts, axis=0))
        a_bd = jnp.concatenate(blocks, axis=1).astype(BF16)
        acc_ref[...] += _dot(a_bd, vals)

    @pl.when(p == 0)
    def _():
        acc_ref[...] = jnp.zeros_like(acc_ref)
        run_ref[...] = jnp.zeros_like(run_ref)
        qry = lax.broadcasted_iota(jnp.int32, (ncol, page), 0) % SUBLANES
        key = lax.broadcasted_iota(jnp.int32, (ncol, page), 1)
        by_head = lambda r: jnp.concatenate([r[:, h * q.shape[1]:(h + 1) * q.shape[1]] for h in range(n_heads)], axis=0)
        block(by_head(kn_ref[...]), by_head(vn_ref[...]), (key < qry) & (qry < n_new))

    def cached(ref):
        heads = [ref[pl.ds(h, page, stride=n_heads), :] for h in range(n_heads)]
        return jnp.concatenate(heads, axis=0).astype(BF16)

    for kp_ref, vp_ref in zip(kp_refs, vp_refs):
        block(cached(kp_ref), cached(vp_ref), None)

    @pl.when(p == pl.num_programs(1) - 1)
    def _():
        o_ref[...] = acc_ref[...]


def _sb_sample(page_table, q, bias_cols, k_new, v_new, cache_k, cache_v, first_page, n_heads, n_new,
               pages_per_step):
    nb, n_pages = page_table.shape
    ncol, dh = q.shape[1], q.shape[2]
    w = n_heads * dh
    rows = cache_k.shape[1]
    page = rows // n_heads
    last = n_pages - 1

    def page_spec(i):
        return pl.BlockSpec((None, rows, dh),
                            lambda b, p, pt: (pt[b, last - (p * pages_per_step + i)] + first_page, 0, 0))

    grid_spec = pltpu.PrefetchScalarGridSpec(
        num_scalar_prefetch=1,
        grid=(nb, n_pages // pages_per_step),
        in_specs=[
            pl.BlockSpec((None, ncol, dh), lambda b, p, pt: (b, 0, 0)),
            pl.BlockSpec((ncol, 1), lambda b, p, pt: (0, 0)),
            pl.BlockSpec((None, page, w), lambda b, p, pt: (b, 0, 0)),
            pl.BlockSpec((None, page, w), lambda b, p, pt: (b, 0, 0)),
        ] + [page_spec(i) for i in range(pages_per_step)] * 2,
        out_specs=pl.BlockSpec((None, ncol, dh), lambda b, p, pt: (b, 0, 0)),
        scratch_shapes=[pltpu.VMEM((ncol, dh), F32), pltpu.VMEM((ncol, 1), F32)],
    )
    return pl.pallas_call(
        functools.partial(_sb_sample_kernel, n_heads=n_heads, n_new=n_new,
                          pages_per_step=pages_per_step),
        out_shape=jax.ShapeDtypeStruct((nb, ncol, dh), F32),
        grid_spec=grid_spec,
        compiler_params=_params("arbitrary", "arbitrary"),
        name="sb_sample",
    )(page_table, q, bias_cols, k_new, v_new, *([cache_k] * pages_per_step), *([cache_v] * pages_per_step))


def _mlstm_kernel(*refs, n_heads, n_valid, has_init):
    if has_init:
        (q_ref, k_ref, v_ref, o_ref, g_ref, bg_ref, gh_ref, c0_ref, n0_ref, m0_ref,
         y_ref, c_ref, n_ref, m_ref) = refs
    else:
        q_ref, k_ref, v_ref, o_ref, g_ref, bg_ref, gh_ref, y_ref, c_ref, n_ref, m_ref = refs
    chunk = pl.program_id(1)
    l = q_ref.shape[0]
    dk = q_ref.shape[1] // n_heads
    dv = v_ref.shape[1] // n_heads

    @pl.when(chunk == 0)
    def _():
        if has_init:
            c_ref[...] = c0_ref[...]
            n_ref[...] = n0_ref[...]
            m_ref[...] = m0_ref[...]
        else:
            c_ref[...] = jnp.zeros_like(c_ref)
            n_ref[...] = jnp.zeros_like(n_ref)
            m_ref[...] = jnp.zeros_like(m_ref)

    gates = g_ref[...] + bg_ref[...]
    lane = lax.broadcasted_iota(jnp.int32, gates.shape, 1)
    tok = lax.broadcasted_iota(jnp.int32, gates.shape, 0)
    is_forget = (lane >= n_heads) & (lane < 2 * n_heads)
    log_f = jnp.where(is_forget, -_softplus_log1p(-gates), 0.0)
    if n_valid < l:
        log_f = jnp.where(tok < n_valid, log_f, 0.0)
        gates = jnp.where(tok < n_valid, gates, -jnp.inf)
    ti = lax.broadcasted_iota(jnp.int32, (l, l), 0)
    si = lax.broadcasted_iota(jnp.int32, (l, l), 1)
    causal = si <= ti
    diag = si == ti
    tri = causal.astype(BF16)
    cum_col = _split_dot(log_f, tri, 3, left=True)

    def to_row(column):
        return jnp.sum(jnp.where(diag, column, 0.0), axis=0, keepdims=True)

    for h in range(n_heads):
        q = q_ref[:, h * dk:(h + 1) * dk]
        k = k_ref[:, h * dk:(h + 1) * dk]
        v = v_ref[:, h * dv:(h + 1) * dv]
        c = c_ref[h]
        n = n_ref[h:h + 1, :]
        m = m_ref[h:h + 1, 0:1]
        b_col = cum_col[:, n_heads + h:n_heads + h + 1]
        ig_col = gates[:, h:h + 1]
        b_row = to_row(b_col)
        ig_row = to_row(ig_col)

        d = jnp.where(causal, b_col - b_row + ig_row, -jnp.inf)
        inter = b_col + m
        m_t = jnp.maximum(inter, jnp.max(d, axis=1, keepdims=True))
        dw = jnp.exp(d - m_t)
        wi = jnp.exp(inter - m_t)
        s = _dot_nt(q, k) * (dw * dk ** -0.5)
        num = wi * _dot_nt(q, c.astype(BF16)) + _dot(s.astype(BF16), v)
        qn = jnp.sum(q.astype(F32) * n.astype(BF16).astype(F32), axis=1, keepdims=True)
        den = wi * qn + jnp.sum(s, axis=1, keepdims=True)
        hid = num / jnp.maximum(jnp.abs(den), jnp.exp(-m_t))

        b_last = b_col[l - 1:l, :]
        g_col = b_last - b_col + ig_col
        m_new = jnp.maximum(b_last + m, jnp.max(g_col, axis=0, keepdims=True))
        w_c = jnp.exp(b_last + m - m_new)
        w_s = jnp.exp(g_col - m_new)
        kw = w_s * (k.astype(F32) * dk ** -0.5)
        c_ref[h] = w_c * c + _dot_tn(v, kw.astype(BF16))
        n_ref[h:h + 1, :] = w_c * n + jnp.sum(kw, axis=0, keepdims=True)
        m_ref[h:h + 1, :] = jnp.broadcast_to(m_new, (1, m_ref.shape[1]))

        hid = hid * lax.rsqrt(jnp.mean(hid * hid, axis=1, keepdims=True) + HEAD_EPS)
        gate = jax.nn.sigmoid(o_ref[:, h * dv:(h + 1) * dv])
        y_ref[:, h * dv:(h + 1) * dv] = (hid * gh_ref[:, h * dv:(h + 1) * dv] * gate).astype(BF16)


def _mlstm(q, k, v, o, gates, b_gate, g_head, n_heads, l, n_valid, init=None):
    nb, t, qk_w = q.shape
    v_w = v.shape[2]
    dk = qk_w // n_heads
    dv = v_w // n_heads
    tok = lambda width: pl.BlockSpec((None, l, width), lambda b, c: (b, c, 0))
    const = lambda shape: pl.BlockSpec(shape, lambda b, c: (0,) * len(shape))
    c_spec = pl.BlockSpec((None, n_heads, dv, dk), lambda b, c: (b, 0, 0, 0))
    n_spec = pl.BlockSpec((None, n_heads, dk), lambda b, c: (b, 0, 0))
    m_spec = pl.BlockSpec((None, n_heads, LANES), lambda b, c: (b, 0, 0))
    in_specs = [tok(qk_w), tok(qk_w), tok(v_w), tok(v_w), tok(LANES), const((1, LANES)), const((1, v_w))]
    args = [q, k, v, o, gates, b_gate, g_head]
    if init is not None:
        in_specs += [c_spec, n_spec, m_spec]
        args += list(init)
    return pl.pallas_call(
        functools.partial(_mlstm_kernel, n_heads=n_heads, n_valid=n_valid, has_init=init is not None),
        out_shape=(
            jax.ShapeDtypeStruct((nb, t, v_w), BF16),
            jax.ShapeDtypeStruct((nb, n_heads, dv, dk), F32),
            jax.ShapeDtypeStruct((nb, n_heads, dk), F32),
            jax.ShapeDtypeStruct((nb, n_heads, LANES), F32),
        ),
        grid=(nb, t // l),
        in_specs=in_specs,
        out_specs=[tok(v_w), c_spec, n_spec, m_spec],
        compiler_params=_params("arbitrary", "arbitrary"),
        name="mlstm",
    )(*args)


def _pick(n, candidates):
    for c in candidates:
        if n % c == 0:
            return c
    return n


def kernel(x_prompt, x_sample, cache_k, cache_v, page_table, state_C, state_n, state_m, state_conv, w_in_a, w_out_a, sb_bias, w_in_b, b_gate_b, g_head_b, w_out_b, w_up, conv_w, conv_b, w_down, ln1_g, ln1_b, ln2_g, ln2_b):
    depth = w_up.shape[0]
    alpha = (2 * depth) ** 0.25
    bp, seq, d_model = x_prompt.shape
    bs, t_new, _ = x_sample.shape
    n_heads_a = sb_bias.shape[1]
    w_a = w_in_a.shape[2] // 3
    dh_a = w_a // n_heads_a
    n_heads_b = state_C.shape[2]
    dv_b, dk_b = state_C.shape[3], state_C.shape[4]
    qk_w, v_w = n_heads_b * dk_b, n_heads_b * dv_b
    d_ff = w_up.shape[2] // 2
    page = cache_k.shape[2]
    assert bs == SUBLANES and t_new <= SUBLANES and w_a == d_model

    mp = bp * seq
    ms = bs * t_new
    tm_p = _pick(seq, TM_PROMPT)
    tn_f = _pick(d_ff, TN_FF)

    xp = x_prompt.reshape(mp, d_model)
    xs = x_sample.transpose(1, 0, 2).reshape(ms, d_model)

    def to_batch_major(a, pad_to=None):
        a = a.reshape(t_new, bs, -1).transpose(1, 0, 2)
        if pad_to is not None:
            a = jnp.pad(a, ((0, 0), (0, pad_to - t_new), (0, 0)))
        return a

    def to_time_major(a):
        return a[:, :t_new].transpose(1, 0, 2).reshape(ms, -1)

    outs_p = dict(k=[], v=[], c=[], n=[], m=[], conv=[])
    outs_s = dict(k=[], v=[], c=[], n=[], m=[], conv=[])
    xp_f, xs_f = xp, xs
    xp_b = xs_b = None

    for i in range(depth):
        j = i // N_MIXERS
        if i % N_MIXERS == 0:
            w_in = w_in_a[j].astype(BF16)
            w_out = w_out_a[j].astype(BF16)
            qb, kb, vb, kf, vf = _qkv_proj(xp_f, w_in, dh_a ** -0.5, tm_p, TN_QKV)
            outs_p["k"].append(kf.reshape(bp, seq, n_heads_a, dh_a))
            outs_p["v"].append(vf.reshape(bp, seq, n_heads_a, dh_a))
            shp = (bp, seq, w_a)
            t_attn = _pick(seq, T_ATTN)
            n_sub = _pick(seq // t_attn, ATTN_SUBTILES)
            mix_p = _sb_prompt(qb.reshape(shp), kb.reshape(shp), vb.reshape(shp), sb_bias[j], n_heads_a,
                               t_attn, n_sub).reshape(mp, w_a)
            qb, kb, vb, kf, vf = _qkv_proj(xs_f, w_in, dh_a ** -0.5, ms, TN_QKV)
            outs_s["k"].append(to_batch_major(kf).reshape(bs, t_new, n_heads_a, dh_a))
            outs_s["v"].append(to_batch_major(vf).reshape(bs, t_new, n_heads_a, dh_a))
            q8 = to_batch_major(qb, SUBLANES).reshape(bs, SUBLANES, n_heads_a, dh_a)
            q_hq = q8.transpose(0, 2, 1, 3).reshape(bs, n_heads_a * SUBLANES, dh_a)
            bias_cols = jnp.repeat(sb_bias[j], SUBLANES)[:, None]
            n_layers_a, pool = cache_k.shape[0], cache_k.shape[1]
            as_stored = lambda c: c.reshape(n_layers_a * pool, page * n_heads_a, dh_a)
            o_s = _sb_sample(page_table, q_hq, bias_cols, to_batch_major(kb, page), to_batch_major(vb, page),
                             as_stored(cache_k), as_stored(cache_v), j * pool, n_heads_a, t_new,
                             _pick(page_table.shape[1], PAGES_PER_STEP))
            o_s = o_s.reshape(bs, n_heads_a, SUBLANES, dh_a).transpose(0, 2, 1, 3).reshape(bs, SUBLANES, w_a)
            mix_s = to_time_major(o_s).astype(BF16)
        else:
            w_in = w_in_b[j].astype(BF16)
            w_out = w_out_b[j].astype(BF16)
            n_main = 2 * qk_w + 2 * v_w
            w_gate = jnp.pad(w_in[:, n_main:], ((0, 0), (0, LANES - 2 * n_heads_b)))
            b_gate = jnp.pad(b_gate_b[j], (0, LANES - 2 * n_heads_b))[None, :]
            g_head = g_head_b[j][None, :]
            q, k, v, o, g = _mlstm_proj(xp_b, w_in, w_gate, qk_w, v_w, tm_p, 4)
            r3 = lambda a: a.reshape(bp, seq, -1)
            l = MLSTM_CHUNK if seq % MLSTM_CHUNK == 0 else seq
            mix_p, c, n, m = _mlstm(r3(q), r3(k), r3(v), r3(o), r3(g), b_gate, g_head, n_heads_b, l, l)
            mix_p = mix_p.reshape(mp, v_w)
            outs_p["c"].append(c)
            outs_p["n"].append(n)
            outs_p["m"].append(m[:, :, 0])
            q, k, v, o, g = _mlstm_proj(xs_b, w_in, w_gate, qk_w, v_w, ms, 4)
            ls = MLSTM_CHUNK if t_new % MLSTM_CHUNK == 0 else t_new
            assert ls == t_new and t_new <= MLSTM_CHUNK
            padl = lambda a: to_batch_major(a, MLSTM_CHUNK)
            m0 = jnp.broadcast_to(state_m[j][:, :, None], (bs, n_heads_b, LANES))
            y, c, n, m = _mlstm(padl(q), padl(k), padl(v), padl(o), padl(g), b_gate, g_head, n_heads_b,
                                MLSTM_CHUNK, t_new, init=(state_C[j], state_n[j], m0))
            mix_s = to_time_major(y)
            outs_s["c"].append(c)
            outs_s["n"].append(n)
            outs_s["m"].append(m[:, :, 0])

        g1, b1 = ln1_g[i][None, :], ln1_b[i][None, :]
        g2, b2 = ln2_g[i][None, :], ln2_b[i][None, :]
        w_u = w_up[i].astype(BF16)
        w_d = w_down[i].astype(BF16)
        cw, cb = conv_w[i], conv_b[i][None, :]

        xp_f, xp_b = _proj_ln(mix_p, w_out, xp_f, g1, b1, alpha, _pick(seq, TM_LN))
        gp, conv_p = _ffn_up_prompt(xp_b, w_u, cw, cb, bp, tm_p, tn_f)
        outs_p["conv"].append(conv_p)
        xp_f, xp_b = _proj_ln(gp, w_d, xp_f, g2, b2, alpha, _pick(seq, TM_DOWN))

        xs_f, xs_b = _proj_ln(mix_s, w_out, xs_f, g1, b1, alpha, ms)
        gs, conv_s = _ffn_up_sample(xs_b, w_u, cw, cb, state_conv[i].transpose(1, 0, 2), t_new, tn_f)
        outs_s["conv"].append(conv_s.transpose(1, 0, 2))
        xs_f, xs_b = _proj_ln(gs, w_d, xs_f, g2, b2, alpha, ms)

    y_prompt = xp_f.reshape(bp, seq, d_model)
    y_sample = xs_f.reshape(t_new, bs, d_model).transpose(1, 0, 2)
    st = jnp.stack
    return (y_prompt, y_sample, st(outs_p["k"]), st(outs_p["v"]), st(outs_s["k"]), st(outs_s["v"]),
            st(outs_p["c"]), st(outs_p["n"]), st(outs_p["m"]), st(outs_s["c"]), st(outs_s["n"]), st(outs_s["m"]),
            st(outs_p["conv"]), st(outs_s["conv"]))
```

```python
import functools

import jax
import jax.numpy as jnp
from jax import lax
from jax.experimental import pallas as pl
from jax.experimental.pallas import tpu as pltpu

F32 = jnp.float32
BF16 = jnp.bfloat16

LN_EPS = 1e-5
HEAD_EPS = 1e-6
CONV_W = 3
MLSTM_CHUNK = 128
N_MIXERS = 2
LOG2_E = 1.4426950408889634

SUBLANES = 8
LANES = 128
MXU_DIM = 256
VMEM_LIMIT_BYTES = 56 * 1024 * 1024

TM_PROMPT = (1024, 512, 256, 128)
TM_LN = (512, 256, 128)
TM_DOWN = (256, 128)
TN_FF = (512, 256, 128)
TN_QKV = 256
T_ATTN = (MXU_DIM, LANES)
FFN_ROWS = (256, 128)
ATTN_SUBTILES = (4, 2, 1)
PAGES_PER_STEP = (4, 2, 1)


def _params(*semantics):
    return pltpu.CompilerParams(dimension_semantics=semantics, vmem_limit_bytes=VMEM_LIMIT_BYTES)


def _dot(a, b):
    return jnp.dot(a, b, preferred_element_type=F32)


def _dot_nt(a, b):
    return lax.dot_general(a, b, (((1,), (1,)), ((), ())), preferred_element_type=F32)


def _dot_tn(a, b):
    return lax.dot_general(a, b, (((0,), (0,)), ((), ())), preferred_element_type=F32)


def _split_dot(x, t, n_terms, left=False):
    acc = None
    rem = x
    for i in range(n_terms):
        part = rem.astype(BF16)
        term = _dot(t, part) if left else _dot(part, t)
        acc = term if acc is None else acc + term
        if i + 1 < n_terms:
            rem = rem - part.astype(F32)
    return acc


def _softplus2(z):
    return jnp.maximum(z, 0.0) + jnp.log2(1.0 + jnp.exp2(-jnp.abs(z)))


def _softplus_log1p(z):
    return jnp.maximum(z, 0.0) + jnp.log1p(jnp.exp(-jnp.abs(z)))


def _layer_norm_rows(z, g, b):
    mu = jnp.mean(z, axis=-1, keepdims=True)
    zc = z - mu
    var = jnp.mean(zc * zc, axis=-1, keepdims=True)
    return zc * lax.rsqrt(var + LN_EPS) * g + b


def _qkv_kernel(x_ref, wq_ref, wk_ref, wv_ref, qb_ref, kb_ref, vb_ref, kf_ref, vf_ref, xb_ref, *, q_scale):
    @pl.when(pl.program_id(1) == 0)
    def _():
        xb_ref[...] = x_ref[...].astype(BF16)

    xb = xb_ref[...]
    qb_ref[...] = (_dot(xb, wq_ref[...]) * q_scale).astype(BF16)
    k = _dot(xb, wk_ref[...])
    kf_ref[...] = k
    kb_ref[...] = k.astype(BF16)
    v = _dot(xb, wv_ref[...])
    vf_ref[...] = v
    vb_ref[...] = v.astype(BF16)


def _qkv_proj(x, w, layer, q_scale, tm, tn):
    m, d = x.shape
    wa = w.shape[2] // 3
    nj = wa // tn
    row = lambda i, j: (i, j)
    out_bf = jax.ShapeDtypeStruct((m, wa), BF16)
    out_f = jax.ShapeDtypeStruct((m, wa), F32)
    return pl.pallas_call(
        functools.partial(_qkv_kernel, q_scale=q_scale),
        out_shape=(out_bf, out_bf, out_bf, out_f, out_f),
        grid=(m // tm, nj),
        in_specs=[
            pl.BlockSpec((tm, d), lambda i, j: (i, 0)),
            pl.BlockSpec((None, d, tn), lambda i, j: (layer, 0, j)),
            pl.BlockSpec((None, d, tn), lambda i, j: (layer, 0, nj + j)),
            pl.BlockSpec((None, d, tn), lambda i, j: (layer, 0, 2 * nj + j)),
        ],
        out_specs=[pl.BlockSpec((tm, tn), row)] * 5,
        scratch_shapes=[pltpu.VMEM((tm, d), BF16)],
        compiler_params=_params("arbitrary", "arbitrary"),
        name="qkv_proj",
    )(x, w, w, w)


def _mlstm_proj_kernel(x_ref, wq_ref, wk_ref, wv_ref, wo_ref, wg_ref, q_ref, k_ref, v_ref, o_ref, g_ref):
    xb = x_ref[...]
    q_ref[...] = _dot(xb, wq_ref[...]).astype(BF16)
    k_ref[...] = _dot(xb, wk_ref[...]).astype(BF16)
    v_ref[...] = _dot(xb, wv_ref[...]).astype(BF16)
    o_ref[...] = _dot(xb, wo_ref[...])

    @pl.when(pl.program_id(1) == 0)
    def _():
        g_ref[...] = _dot(xb, wg_ref[...])


def _mlstm_proj(xb, w, layer, w_gate, qk_w, v_w, tm, nj):
    m, d = xb.shape
    tq = qk_w // nj
    tv = v_w // nj
    return pl.pallas_call(
        _mlstm_proj_kernel,
        out_shape=(
            jax.ShapeDtypeStruct((m, qk_w), BF16),
            jax.ShapeDtypeStruct((m, qk_w), BF16),
            jax.ShapeDtypeStruct((m, v_w), BF16),
            jax.ShapeDtypeStruct((m, v_w), F32),
            jax.ShapeDtypeStruct((m, LANES), F32),
        ),
        grid=(m // tm, nj),
        in_specs=[
            pl.BlockSpec((tm, d), lambda i, j: (i, 0)),
            pl.BlockSpec((None, d, tq), lambda i, j: (layer, 0, j)),
            pl.BlockSpec((None, d, tq), lambda i, j: (layer, 0, nj + j)),
            pl.BlockSpec((None, d, tv), lambda i, j: (layer, 0, (2 * qk_w) // tv + j)),
            pl.BlockSpec((None, d, tv), lambda i, j: (layer, 0, (2 * qk_w + v_w) // tv + j)),
            pl.BlockSpec((d, LANES), lambda i, j: (0, 0)),
        ],
        out_specs=[
            pl.BlockSpec((tm, tq), lambda i, j: (i, j)),
            pl.BlockSpec((tm, tq), lambda i, j: (i, j)),
            pl.BlockSpec((tm, tv), lambda i, j: (i, j)),
            pl.BlockSpec((tm, tv), lambda i, j: (i, j)),
            pl.BlockSpec((tm, LANES), lambda i, j: (i, 0)),
        ],
        compiler_params=_params("arbitrary", "arbitrary"),
        name="mlstm_proj",
    )(xb, w, w, w, w, w_gate)


def _proj_ln_kernel(a_ref, w_ref, r_ref, g_ref, b_ref, of_ref, ob_ref, *, alpha):
    y = _dot(a_ref[...], w_ref[...])
    out = _layer_norm_rows(alpha * r_ref[...] + y, g_ref[...], b_ref[...])
    of_ref[...] = out
    ob_ref[...] = out.astype(BF16)


def _proj_ln(a, w, layer, resid, g, b, alpha, tm):
    m, kd = a.shape
    d = w.shape[2]
    return pl.pallas_call(
        functools.partial(_proj_ln_kernel, alpha=alpha),
        out_shape=(jax.ShapeDtypeStruct((m, d), F32), jax.ShapeDtypeStruct((m, d), BF16)),
        grid=(m // tm,),
        in_specs=[
            pl.BlockSpec((tm, kd), lambda i: (i, 0)),
            pl.BlockSpec((None, kd, d), lambda i: (layer, 0, 0), pipeline_mode=pl.Buffered(1)),
            pl.BlockSpec((tm, d), lambda i: (i, 0)),
            pl.BlockSpec((1, d), lambda i: (0, 0)),
            pl.BlockSpec((1, d), lambda i: (0, 0)),
        ],
        out_specs=[pl.BlockSpec((tm, d), lambda i: (i, 0))] * 2,
        compiler_params=_params("arbitrary"),
        name="proj_ln",
    )(a, w, resid, g, b)


def _silu_gate(cg, cv):
    return (cg * jax.nn.sigmoid(cg) * cv).astype(BF16)


def _ffn_up_prompt_kernel(x_ref, wg_ref, wv_ref, cwg_ref, cwv_ref, cbg_ref, cbv_ref,
                          o_ref, sg_ref, sv_ref, carry_ref, *, tiles_per_seq, rows):
    i = pl.program_id(0)
    j = pl.program_id(1)
    nj = pl.num_programs(1)
    tm = x_ref.shape[0]
    tn = o_ref.shape[1]

    @pl.when((i % tiles_per_seq) == 0)
    def _():
        carry_ref[j] = jnp.zeros((SUBLANES, tn), F32)
        carry_ref[nj + j] = jnp.zeros((SUBLANES, tn), F32)

    rowi = lax.broadcasted_iota(jnp.int32, (SUBLANES, tn), 0)

    def conv(u, prev, cw_ref, cb_ref):
        w0 = cw_ref[0:1, :]
        w1 = cw_ref[1:2, :]
        w2 = cw_ref[2:3, :]
        bias = cb_ref[...]
        c = bias + w0 * pltpu.roll(u, 2, 0) + w1 * pltpu.roll(u, 1, 0) + w2 * u
        u8 = u[:SUBLANES, :]
        p0 = prev[SUBLANES - 2:SUBLANES - 1, :]
        p1 = prev[SUBLANES - 1:SUBLANES, :]
        m1 = jnp.where(rowi == 0, p1, pltpu.roll(u8, 1, 0))
        m2 = jnp.where(rowi == 0, p0, jnp.where(rowi == 1, p1, pltpu.roll(u8, 2, 0)))
        c8 = bias + w0 * m2 + w1 * m1 + w2 * u8
        return jnp.concatenate([c8, c[SUBLANES:, :]], axis=0)

    prev_g = carry_ref[j]
    prev_v = carry_ref[nj + j]
    for r in range(0, tm, rows):
        xb = x_ref[r:r + rows, :]
        ug = _dot(xb, wg_ref[...])
        uv = _dot(xb, wv_ref[...])
        o_ref[r:r + rows, :] = _silu_gate(conv(ug, prev_g, cwg_ref, cbg_ref), conv(uv, prev_v, cwv_ref, cbv_ref))
        prev_g = ug[rows - SUBLANES:, :]
        prev_v = uv[rows - SUBLANES:, :]
    carry_ref[j] = prev_g
    carry_ref[nj + j] = prev_v
    sg_ref[...] = prev_g[SUBLANES - (CONV_W - 1):, :]
    sv_ref[...] = prev_v[SUBLANES - (CONV_W - 1):, :]


def _ffn_up_prompt(xb, w_up, layer, conv_w, conv_b, batch, tm, tn, rows):
    m, d = xb.shape
    f = w_up.shape[2] // 2
    nj = f // tn
    seq = m // batch
    tiles_per_seq = seq // tm
    state = jax.ShapeDtypeStruct((m // tm, CONV_W - 1, f), F32)
    g, sg, sv = pl.pallas_call(
        functools.partial(_ffn_up_prompt_kernel, tiles_per_seq=tiles_per_seq, rows=rows),
        out_shape=(jax.ShapeDtypeStruct((m, f), BF16), state, state),
        grid=(m // tm, nj),
        in_specs=[
            pl.BlockSpec((tm, d), lambda i, j: (i, 0)),
            pl.BlockSpec((None, d, tn), lambda i, j: (layer, 0, j)),
            pl.BlockSpec((None, d, tn), lambda i, j: (layer, 0, nj + j)),
            pl.BlockSpec((CONV_W, tn), lambda i, j: (0, j)),
            pl.BlockSpec((CONV_W, tn), lambda i, j: (0, nj + j)),
            pl.BlockSpec((1, tn), lambda i, j: (0, j)),
            pl.BlockSpec((1, tn), lambda i, j: (0, nj + j)),
        ],
        out_specs=[
            pl.BlockSpec((tm, tn), lambda i, j: (i, j)),
            pl.BlockSpec((None, CONV_W - 1, tn), lambda i, j: (i, 0, j)),
            pl.BlockSpec((None, CONV_W - 1, tn), lambda i, j: (i, 0, j)),
        ],
        scratch_shapes=[pltpu.VMEM((2 * nj, SUBLANES, tn), F32)],
        compiler_params=_params("arbitrary", "arbitrary"),
        name="ffn_up_prompt",
    )(xb, w_up, w_up, conv_w, conv_w, conv_b, conv_b)
    last = slice(tiles_per_seq - 1, None, tiles_per_seq)
    return g, jnp.concatenate([sg[last], sv[last]], axis=-1)


def _ffn_up_sample_kernel(x_ref, wg_ref, wv_ref, cwg_ref, cwv_ref, cbg_ref, cbv_ref, pg_ref, pv_ref,
                          o_ref, sg_ref, sv_ref, *, n_t, nb):
    xb = x_ref[...]

    def conv(u, cw_ref, cb_ref, p_ref, s_ref):
        up = [p_ref[0], p_ref[1]] + [u[t * nb:(t + 1) * nb, :] for t in range(n_t)]
        s_ref[0] = up[n_t]
        s_ref[1] = up[n_t + 1]
        bias = cb_ref[...]
        return [bias + cw_ref[0:1, :] * up[t] + cw_ref[1:2, :] * up[t + 1] + cw_ref[2:3, :] * up[t + 2]
                for t in range(n_t)]

    cg = conv(_dot(xb, wg_ref[...]), cwg_ref, cbg_ref, pg_ref, sg_ref)
    cv = conv(_dot(xb, wv_ref[...]), cwv_ref, cbv_ref, pv_ref, sv_ref)
    for t in range(n_t):
        o_ref[t * nb:(t + 1) * nb, :] = _silu_gate(cg[t], cv[t])


def _ffn_up_sample(xb, w_up, layer, conv_w, conv_b, prefix, n_t, tn):
    m, d = xb.shape
    nb = m // n_t
    f = w_up.shape[2] // 2
    nj = f // tn
    state = jax.ShapeDtypeStruct((CONV_W - 1, nb, f), F32)
    g, sg, sv = pl.pallas_call(
        functools.partial(_ffn_up_sample_kernel, n_t=n_t, nb=nb),
        out_shape=(jax.ShapeDtypeStruct((m, f), BF16), state, state),
        grid=(nj,),
        in_specs=[
            pl.BlockSpec((m, d), lambda j: (0, 0)),
            pl.BlockSpec((None, d, tn), lambda j: (layer, 0, j)),
            pl.BlockSpec((None, d, tn), lambda j: (layer, 0, nj + j)),
            pl.BlockSpec((CONV_W, tn), lambda j: (0, j)),
            pl.BlockSpec((CONV_W, tn), lambda j: (0, nj + j)),
            pl.BlockSpec((1, tn), lambda j: (0, j)),
            pl.BlockSpec((1, tn), lambda j: (0, nj + j)),
            pl.BlockSpec((CONV_W - 1, nb, tn), lambda j: (0, 0, j)),
            pl.BlockSpec((CONV_W - 1, nb, tn), lambda j: (0, 0, nj + j)),
        ],
        out_specs=[
            pl.BlockSpec((m, tn), lambda j: (0, j)),
            pl.BlockSpec((CONV_W - 1, nb, tn), lambda j: (0, 0, j)),
            pl.BlockSpec((CONV_W - 1, nb, tn), lambda j: (0, 0, j)),
        ],
        compiler_params=_params("arbitrary"),
        name="ffn_up_sample",
    )(xb, w_up, w_up, conv_w, conv_w, conv_b, conv_b, prefix, prefix)
    return g, jnp.concatenate([sg, sv], axis=-1)


def _sb_prompt_kernel(bias_ref, q_ref, k_ref, v_ref, o_ref, acc_ref, run_ref, *, t, n_sub):
    h = pl.program_id(1)
    qi = pl.program_id(2)
    bias = bias_ref[h]
    row = lax.broadcasted_iota(jnp.int32, (t, t), 0)
    col = lax.broadcasted_iota(jnp.int32, (t, t), 1)
    suffix_incl = (row >= col).astype(BF16)
    acc_ref[...] = jnp.zeros_like(acc_ref)
    run_ref[...] = jnp.zeros_like(run_ref)

    def block(j, subs):
        start = pl.multiple_of(j * t, t)
        kj = k_ref[pl.ds(start, t), :]
        vj = v_ref[pl.ds(start, t), :]
        valid = col < row
        zs = [_dot_nt(q_ref[s * t:(s + 1) * t, :], kj) + bias for s, _ in subs]
        sps = [_softplus2(z) for z in zs]
        sps = [jnp.where(valid, sp, 0.0) if diagonal else sp for sp, (_, diagonal) in zip(sps, subs)]
        his = [sp.astype(BF16) for sp in sps]
        los = [(sp - hi.astype(F32)).astype(BF16) for sp, hi in zip(sps, his)]
        incls = [_dot(hi, suffix_incl) + _dot(lo, suffix_incl) for hi, lo in zip(his, los)]
        probs = []
        for (s, diagonal), z, incl in zip(subs, zs, incls):
            a = jnp.exp2(z - incl - run_ref[s])
            probs.append((jnp.where(valid, a, 0.0) if diagonal else a).astype(BF16))
            run_ref[s] += incl[:, 0:1]
        outs = [_dot(a, vj) for a in probs]
        for (s, _), out in zip(subs, outs):
            acc_ref[s] += out

    for r in range(n_sub - 1, -1, -1):
        block(n_sub * qi + r, [(s, s == r) for s in range(r, n_sub)])

    def body(it, carry):
        block(n_sub * qi - 1 - it, [(s, False) for s in range(n_sub)])
        return carry

    lax.fori_loop(0, n_sub * qi, body, 0)
    for s in range(n_sub):
        o_ref[s * t:(s + 1) * t, :] = acc_ref[s].astype(BF16)


def _sb_prompt(q, k, v, bias, n_heads, t, n_sub):
    b, s, w = q.shape
    dh = w // n_heads
    tq = t * n_sub
    return pl.pallas_call(
        functools.partial(_sb_prompt_kernel, t=t, n_sub=n_sub),
        out_shape=jax.ShapeDtypeStruct((b, s, w), BF16),
        grid=(b, n_heads, s // tq),
        in_specs=[
            pl.BlockSpec(memory_space=pltpu.SMEM),
            pl.BlockSpec((None, tq, dh), lambda bi, h, qi: (bi, qi, h)),
            pl.BlockSpec((None, s, dh), lambda bi, h, qi: (bi, 0, h)),
            pl.BlockSpec((None, s, dh), lambda bi, h, qi: (bi, 0, h)),
        ],
        out_specs=pl.BlockSpec((None, tq, dh), lambda bi, h, qi: (bi, qi, h)),
        scratch_shapes=[pltpu.VMEM((n_sub, t, dh), F32), pltpu.VMEM((n_sub, t, 1), F32)],
        compiler_params=_params("arbitrary", "arbitrary", "arbitrary"),
        name="sb_prompt",
    )(bias, q, k, v)


def _sb_sample_kernel(pt_ref, q_ref, bias_ref, kn_ref, vn_ref, *refs, n_heads, n_new, pages_per_step):
    kp_refs = refs[:pages_per_step]
    vp_refs = refs[pages_per_step:2 * pages_per_step]
    o_ref, acc_ref, run_ref = refs[2 * pages_per_step:]
    p = pl.program_id(1)
    page = kp_refs[0].shape[0] // n_heads
    q = q_ref[...]
    bias = bias_ref[...]
    ncol = q.shape[0]
    row = lax.broadcasted_iota(jnp.int32, (page, page), 0)
    col = lax.broadcasted_iota(jnp.int32, (page, page), 1)
    suffix_incl = (row >= col).astype(BF16)

    def head_rows(a, h):
        return a[h * SUBLANES:(h + 1) * SUBLANES]

    def blocks(keys, vals, valid):
        zts = [_dot_nt(q, k) for k in keys]
        zs = [jnp.concatenate([head_rows(zt, h)[:, h * page:(h + 1) * page] for h in range(n_heads)], axis=0)
              + bias for zt in zts]
        sps = [_softplus2(z) for z in zs]
        if valid is not None:
            sps = [jnp.where(valid, sp, 0.0) for sp in sps]
        his = [sp.astype(BF16) for sp in sps]
        los = [(sp - hi.astype(F32)).astype(BF16) for sp, hi in zip(sps, his)]
        incls = [_dot(hi, suffix_incl) + _dot(lo, suffix_incl) for hi, lo in zip(his, los)]
        run = run_ref[...]
        a_bds = []
        for z, incl in zip(zs, incls):
            a = jnp.exp2(z - incl - run)
            if valid is not None:
                a = jnp.where(valid, a, 0.0)
            run = run + incl[:, 0:1]
            cols = []
            for h in range(n_heads):
                parts = []
                if h > 0:
                    parts.append(jnp.zeros((h * SUBLANES, page), F32))
                parts.append(head_rows(a, h))
                if h + 1 < n_heads:
                    parts.append(jnp.zeros((ncol - (h + 1) * SUBLANES, page), F32))
                cols.append(jnp.concatenate(parts, axis=0))
            a_bds.append(jnp.concatenate(cols, axis=1).astype(BF16))
        run_ref[...] = run
        outs = [_dot(a_bd, v) for a_bd, v in zip(a_bds, vals)]
        acc_ref[...] += functools.reduce(lambda x, y: x + y, outs)

    @pl.when(p == 0)
    def _():
        acc_ref[...] = jnp.zeros_like(acc_ref)
        run_ref[...] = jnp.zeros_like(run_ref)
        qry = lax.broadcasted_iota(jnp.int32, (ncol, page), 0) % SUBLANES
        key = lax.broadcasted_iota(jnp.int32, (ncol, page), 1)
        by_head = lambda r: jnp.concatenate([r[:, h * q.shape[1]:(h + 1) * q.shape[1]] for h in range(n_heads)], axis=0)
        blocks([by_head(kn_ref[...])], [by_head(vn_ref[...])], (key < qry) & (qry < n_new))

    def cached(ref):
        heads = [ref[pl.ds(h, page, stride=n_heads), :] for h in range(n_heads)]
        return jnp.concatenate(heads, axis=0).astype(BF16)

    blocks([cached(r) for r in kp_refs], [cached(r) for r in vp_refs], None)

    @pl.when(p == pl.num_programs(1) - 1)
    def _():
        o_ref[...] = acc_ref[...]


def _sb_sample(page_table, q, bias_cols, k_new, v_new, cache_k, cache_v, first_page, n_heads, n_new,
               pages_per_step):
    nb, n_pages = page_table.shape
    ncol, dh = q.shape[1], q.shape[2]
    w = n_heads * dh
    rows = cache_k.shape[1]
    page = rows // n_heads
    last = n_pages - 1

    def page_spec(i):
        return pl.BlockSpec((None, rows, dh),
                            lambda b, p, pt: (pt[b, last - (p * pages_per_step + i)] + first_page, 0, 0))

    grid_spec = pltpu.PrefetchScalarGridSpec(
        num_scalar_prefetch=1,
        grid=(nb, n_pages // pages_per_step),
        in_specs=[
            pl.BlockSpec((None, ncol, dh), lambda b, p, pt: (b, 0, 0)),
            pl.BlockSpec((ncol, 1), lambda b, p, pt: (0, 0)),
            pl.BlockSpec((None, page, w), lambda b, p, pt: (b, 0, 0)),
            pl.BlockSpec((None, page, w), lambda b, p, pt: (b, 0, 0)),
        ] + [page_spec(i) for i in range(pages_per_step)] * 2,
        out_specs=pl.BlockSpec((None, ncol, dh), lambda b, p, pt: (b, 0, 0)),
        scratch_shapes=[pltpu.VMEM((ncol, dh), F32), pltpu.VMEM((ncol, 1), F32)],
    )
    return pl.pallas_call(
        functools.partial(_sb_sample_kernel, n_heads=n_heads, n_new=n_new,
                          pages_per_step=pages_per_step),
        out_shape=jax.ShapeDtypeStruct((nb, ncol, dh), F32),
        grid_spec=grid_spec,
        compiler_params=_params("arbitrary", "arbitrary"),
        name="sb_sample",
    )(page_table, q, bias_cols, k_new, v_new, *([cache_k] * pages_per_step), *([cache_v] * pages_per_step))


def _mlstm_kernel(*refs, n_heads, n_valid, has_init):
    if has_init:
        (q_ref, k_ref, v_ref, o_ref, g_ref, bg_ref, gh_ref, c0_ref, n0_ref, m0_ref,
         y_ref, c_ref, n_ref, m_ref) = refs
    else:
        q_ref, k_ref, v_ref, o_ref, g_ref, bg_ref, gh_ref, y_ref, c_ref, n_ref, m_ref = refs
    chunk = pl.program_id(1)
    l = q_ref.shape[0]
    dk = q_ref.shape[1] // n_heads
    dv = v_ref.shape[1] // n_heads

    @pl.when(chunk == 0)
    def _():
        if has_init:
            c_ref[...] = c0_ref[...]
            n_ref[...] = n0_ref[...]
            m_ref[...] = m0_ref[...]
        else:
            c_ref[...] = jnp.zeros_like(c_ref)
            n_ref[...] = jnp.zeros_like(n_ref)
            m_ref[...] = jnp.zeros_like(m_ref)

    gates = g_ref[...] + bg_ref[...]
    lane = lax.broadcasted_iota(jnp.int32, gates.shape, 1)
    tok = lax.broadcasted_iota(jnp.int32, gates.shape, 0)
    is_forget = (lane >= n_heads) & (lane < 2 * n_heads)
    log_f = jnp.where(is_forget, -_softplus_log1p(-gates), 0.0)
    if n_valid < l:
        log_f = jnp.where(tok < n_valid, log_f, 0.0)
        gates = jnp.where(tok < n_valid, gates, -jnp.inf)
    ti = lax.broadcasted_iota(jnp.int32, (l, l), 0)
    si = lax.broadcasted_iota(jnp.int32, (l, l), 1)
    causal = si <= ti
    diag = si == ti
    tri = causal.astype(BF16)
    cum_col = _split_dot(log_f, tri, 3, left=True)

    def to_row(column):
        return jnp.sum(jnp.where(diag, column, 0.0), axis=0, keepdims=True)

    for h in range(n_heads):
        q = q_ref[:, h * dk:(h + 1) * dk]
        k = k_ref[:, h * dk:(h + 1) * dk]
        v = v_ref[:, h * dv:(h + 1) * dv]
        c = c_ref[h]
        n = n_ref[h:h + 1, :]
        m = m_ref[h:h + 1, 0:1]
        b_col = cum_col[:, n_heads + h:n_heads + h + 1]
        ig_col = gates[:, h:h + 1]
        b_row = to_row(b_col)
        ig_row = to_row(ig_col)

        d = jnp.where(causal, b_col - b_row + ig_row, -jnp.inf)
        inter = b_col + m
        m_t = jnp.maximum(inter, jnp.max(d, axis=1, keepdims=True))
        dw = jnp.exp(d - m_t)
        wi = jnp.exp(inter - m_t)
        s = _dot_nt(q, k) * (dw * dk ** -0.5)
        num = wi * _dot_nt(q, c.astype(BF16)) + _dot(s.astype(BF16), v)
        qn = jnp.sum(q.astype(F32) * n.astype(BF16).astype(F32), axis=1, keepdims=True)
        den = wi * qn + jnp.sum(s, axis=1, keepdims=True)
        hid = num / jnp.maximum(jnp.abs(den), jnp.exp(-m_t))

        b_last = b_col[l - 1:l, :]
        g_col = b_last - b_col + ig_col
        m_new = jnp.maximum(b_last + m, jnp.max(g_col, axis=0, keepdims=True))
        w_c = jnp.exp(b_last + m - m_new)
        w_s = jnp.exp(g_col - m_new)
        kw = w_s * (k.astype(F32) * dk ** -0.5)
        c_ref[h] = w_c * c + _dot_tn(v, kw.astype(BF16))
        n_ref[h:h + 1, :] = w_c * n + jnp.sum(kw, axis=0, keepdims=True)
        m_ref[h:h + 1, :] = jnp.broadcast_to(m_new, (1, m_ref.shape[1]))

        hid = hid * lax.rsqrt(jnp.mean(hid * hid, axis=1, keepdims=True) + HEAD_EPS)
        gate = jax.nn.sigmoid(o_ref[:, h * dv:(h + 1) * dv])
        y_ref[:, h * dv:(h + 1) * dv] = (hid * gh_ref[:, h * dv:(h + 1) * dv] * gate).astype(BF16)


def _mlstm(q, k, v, o, gates, b_gate, g_head, n_heads, l, n_valid, init=None):
    nb, t, qk_w = q.shape
    v_w = v.shape[2]
    dk = qk_w // n_heads
    dv = v_w // n_heads
    tok = lambda width: pl.BlockSpec((None, l, width), lambda b, c: (b, c, 0))
    const = lambda shape: pl.BlockSpec(shape, lambda b, c: (0,) * len(shape))
    c_spec = pl.BlockSpec((None, n_heads, dv, dk), lambda b, c: (b, 0, 0, 0))
    n_spec = pl.BlockSpec((None, n_heads, dk), lambda b, c: (b, 0, 0))
    m_spec = pl.BlockSpec((None, n_heads, LANES), lambda b, c: (b, 0, 0))
    in_specs = [tok(qk_w), tok(qk_w), tok(v_w), tok(v_w), tok(LANES), const((1, LANES)), const((1, v_w))]
    args = [q, k, v, o, gates, b_gate, g_head]
    if init is not None:
        in_specs += [c_spec, n_spec, m_spec]
        args += list(init)
    return pl.pallas_call(
        functools.partial(_mlstm_kernel, n_heads=n_heads, n_valid=n_valid, has_init=init is not None),
        out_shape=(
            jax.ShapeDtypeStruct((nb, t, v_w), BF16),
            jax.ShapeDtypeStruct((nb, n_heads, dv, dk), F32),
            jax.ShapeDtypeStruct((nb, n_heads, dk), F32),
            jax.ShapeDtypeStruct((nb, n_heads, LANES), F32),
        ),
        grid=(nb, t // l),
        in_specs=in_specs,
        out_specs=[tok(v_w), c_spec, n_spec, m_spec],
        compiler_params=_params("arbitrary", "arbitrary"),
        name="mlstm",
    )(*args)


def _pick(n, candidates):
    for c in candidates:
        if n % c == 0:
            return c
    return n


def kernel(x_prompt, x_sample, cache_k, cache_v, page_table, state_C, state_n, state_m, state_conv, w_in_a, w_out_a, sb_bias, w_in_b, b_gate_b, g_head_b, w_out_b, w_up, conv_w, conv_b, w_down, ln1_g, ln1_b, ln2_g, ln2_b):
    depth = w_up.shape[0]
    alpha = (2 * depth) ** 0.25
    bp, seq, d_model = x_prompt.shape
    bs, t_new, _ = x_sample.shape
    n_heads_a = sb_bias.shape[1]
    w_a = w_in_a.shape[2] // 3
    dh_a = w_a // n_heads_a
    n_heads_b = state_C.shape[2]
    dv_b, dk_b = state_C.shape[3], state_C.shape[4]
    qk_w, v_w = n_heads_b * dk_b, n_heads_b * dv_b
    d_ff = w_up.shape[2] // 2
    page = cache_k.shape[2]
    assert bs == SUBLANES and t_new <= SUBLANES and w_a == d_model

    mp = bp * seq
    ms = bs * t_new
    tm_p = _pick(seq, TM_PROMPT)
    tn_f = _pick(d_ff, TN_FF)

    xp = x_prompt.reshape(mp, d_model)
    xs = x_sample.transpose(1, 0, 2).reshape(ms, d_model)

    def to_batch_major(a, pad_to=None):
        a = a.reshape(t_new, bs, -1).transpose(1, 0, 2)
        if pad_to is not None:
            a = jnp.pad(a, ((0, 0), (0, pad_to - t_new), (0, 0)))
        return a

    def to_time_major(a):
        return a[:, :t_new].transpose(1, 0, 2).reshape(ms, -1)

    outs_p = dict(k=[], v=[], c=[], n=[], m=[], conv=[])
    outs_s = dict(k=[], v=[], c=[], n=[], m=[], conv=[])
    xp_f, xs_f = xp, xs
    xp_b = xs_b = None

    w_in_a_b, w_out_a_b = w_in_a.astype(BF16), w_out_a.astype(BF16)
    w_in_b_b, w_out_b_b = w_in_b.astype(BF16), w_out_b.astype(BF16)
    w_up_b, w_down_b = w_up.astype(BF16), w_down.astype(BF16)
    q_scale = dh_a ** -0.5 * LOG2_E

    for i in range(depth):
        j = i // N_MIXERS
        if i % N_MIXERS == 0:
            w_out = w_out_a_b
            qb, kb, vb, kf, vf = _qkv_proj(xp_f, w_in_a_b, j, q_scale, tm_p, TN_QKV)
            outs_p["k"].append(kf.reshape(bp, seq, n_heads_a, dh_a))
            outs_p["v"].append(vf.reshape(bp, seq, n_heads_a, dh_a))
            shp = (bp, seq, w_a)
            t_attn = _pick(seq, T_ATTN)
            n_sub = _pick(seq // t_attn, ATTN_SUBTILES)
            bias2 = sb_bias[j] * LOG2_E
            mix_p = _sb_prompt(qb.reshape(shp), kb.reshape(shp), vb.reshape(shp), bias2, n_heads_a,
                               t_attn, n_sub).reshape(mp, w_a)
            qb, kb, vb, kf, vf = _qkv_proj(xs_f, w_in_a_b, j, q_scale, ms, TN_QKV)
            outs_s["k"].append(to_batch_major(kf).reshape(bs, t_new, n_heads_a, dh_a))
            outs_s["v"].append(to_batch_major(vf).reshape(bs, t_new, n_heads_a, dh_a))
            q8 = to_batch_major(qb, SUBLANES).reshape(bs, SUBLANES, n_heads_a, dh_a)
            q_hq = q8.transpose(0, 2, 1, 3).reshape(bs, n_heads_a * SUBLANES, dh_a)
            bias_cols = jnp.repeat(bias2, SUBLANES)[:, None]
            n_layers_a, pool = cache_k.shape[0], cache_k.shape[1]
            as_stored = lambda c: c.reshape(n_layers_a * pool, page * n_heads_a, dh_a)
            o_s = _sb_sample(page_table, q_hq, bias_cols, to_batch_major(kb, page), to_batch_major(vb, page),
                             as_stored(cache_k), as_stored(cache_v), j * pool, n_heads_a, t_new,
                             _pick(page_table.shape[1], PAGES_PER_STEP))
            o_s = o_s.reshape(bs, n_heads_a, SUBLANES, dh_a).transpose(0, 2, 1, 3).reshape(bs, SUBLANES, w_a)
            mix_s = to_time_major(o_s).astype(BF16)
        else:
            w_out = w_out_b_b
            n_main = 2 * qk_w + 2 * v_w
            w_gate = jnp.pad(w_in_b_b[j, :, n_main:], ((0, 0), (0, LANES - 2 * n_heads_b)))
            b_gate = jnp.pad(b_gate_b[j], (0, LANES - 2 * n_heads_b))[None, :]
            g_head = g_head_b[j][None, :]
            q, k, v, o, g = _mlstm_proj(xp_b, w_in_b_b, j, w_gate, qk_w, v_w, tm_p, 4)
            r3 = lambda a: a.reshape(bp, seq, -1)
            l = MLSTM_CHUNK if seq % MLSTM_CHUNK == 0 else seq
            mix_p, c, n, m = _mlstm(r3(q), r3(k), r3(v), r3(o), r3(g), b_gate, g_head, n_heads_b, l, l)
            mix_p = mix_p.reshape(mp, v_w)
            outs_p["c"].append(c)
            outs_p["n"].append(n)
            outs_p["m"].append(m[:, :, 0])
            q, k, v, o, g = _mlstm_proj(xs_b, w_in_b_b, j, w_gate, qk_w, v_w, ms, 4)
            ls = MLSTM_CHUNK if t_new % MLSTM_CHUNK == 0 else t_new
            assert ls == t_new and t_new <= MLSTM_CHUNK
            padl = lambda a: to_batch_major(a, MLSTM_CHUNK)
            m0 = jnp.broadcast_to(state_m[j][:, :, None], (bs, n_heads_b, LANES))
            y, c, n, m = _mlstm(padl(q), padl(k), padl(v), padl(o), padl(g), b_gate, g_head, n_heads_b,
                                MLSTM_CHUNK, t_new, init=(state_C[j], state_n[j], m0))
            mix_s = to_time_major(y)
            outs_s["c"].append(c)
            outs_s["n"].append(n)
            outs_s["m"].append(m[:, :, 0])

        g1, b1 = ln1_g[i][None, :], ln1_b[i][None, :]
        g2, b2 = ln2_g[i][None, :], ln2_b[i][None, :]
        cw, cb = conv_w[i], conv_b[i][None, :]

        xp_f, xp_b = _proj_ln(mix_p, w_out, j, xp_f, g1, b1, alpha, _pick(seq, TM_LN))
        gp, conv_p = _ffn_up_prompt(xp_b, w_up_b, i, cw, cb, bp, tm_p, tn_f, _pick(tm_p, FFN_ROWS))
        outs_p["conv"].append(conv_p)
        xp_f, xp_b = _proj_ln(gp, w_down_b, i, xp_f, g2, b2, alpha, _pick(seq, TM_DOWN))

        xs_f, xs_b = _proj_ln(mix_s, w_out, j, xs_f, g1, b1, alpha, ms)
        gs, conv_s = _ffn_up_sample(xs_b, w_up_b, i, cw, cb, state_conv[i].transpose(1, 0, 2), t_new, tn_f)
        outs_s["conv"].append(conv_s.transpose(1, 0, 2))
        xs_f, xs_b = _proj_ln(gs, w_down_b, i, xs_f, g2, b2, alpha, ms)

    y_prompt = xp_f.reshape(bp, seq, d_model)
    y_sample = xs_f.reshape(t_new, bs, d_model).transpose(1, 0, 2)
    st = jnp.stack
    return (y_prompt, y_sample, st(outs_p["k"]), st(outs_p["v"]), st(outs_s["k"]), st(outs_s["v"]),
            st(outs_p["c"]), st(outs_p["n"]), st(outs_p["m"]), st(outs_s["c"]), st(outs_s["n"]), st(outs_s["m"]),
            st(outs_p["conv"]), st(outs_s["conv"]))
```

```python
import functools

import jax
import jax.numpy as jnp
from jax import lax
from jax.experimental import pallas as pl
from jax.experimental.pallas import tpu as pltpu

F32 = jnp.float32
BF16 = jnp.bfloat16

LN_EPS = 1e-5
HEAD_EPS = 1e-6
CONV_W = 3
MLSTM_CHUNK = 128
N_MIXERS = 2
LOG2_E = 1.4426950408889634

SUBLANES = 8
LANES = 128
MXU_DIM = 256
VMEM_LIMIT_BYTES = 56 * 1024 * 1024

TM_PROMPT = (1024, 512, 256, 128)
TM_LN = (512, 256, 128)
TM_DOWN = (256, 128)
TN_FF = (512, 256, 128)
TN_QKV = 256
T_ATTN = (MXU_DIM, LANES)
FFN_ROWS = (256, 128)
ATTN_SUBTILES = (4, 2, 1)
PAGES_PER_STEP = (4, 2, 1)


def _params(*semantics):
    return pltpu.CompilerParams(dimension_semantics=semantics, vmem_limit_bytes=VMEM_LIMIT_BYTES)


def _dot(a, b):
    return jnp.dot(a, b, preferred_element_type=F32)


def _dot_nt(a, b):
    return lax.dot_general(a, b, (((1,), (1,)), ((), ())), preferred_element_type=F32)


def _dot_tn(a, b):
    return lax.dot_general(a, b, (((0,), (0,)), ((), ())), preferred_element_type=F32)


def _split_dot(x, t, n_terms, left=False):
    acc = None
    rem = x
    for i in range(n_terms):
        part = rem.astype(BF16)
        term = _dot(t, part) if left else _dot(part, t)
        acc = term if acc is None else acc + term
        if i + 1 < n_terms:
            rem = rem - part.astype(F32)
    return acc


def _softplus2(z):
    return jnp.maximum(z, 0.0) + jnp.log2(1.0 + jnp.exp2(-jnp.abs(z)))


def _softplus_log1p(z):
    return jnp.maximum(z, 0.0) + jnp.log1p(jnp.exp(-jnp.abs(z)))


def _layer_norm_rows(z, g, b):
    mu = jnp.mean(z, axis=-1, keepdims=True)
    zc = z - mu
    var = jnp.mean(zc * zc, axis=-1, keepdims=True)
    return zc * lax.rsqrt(var + LN_EPS) * g + b


def _qkv_kernel(x_ref, wq_ref, wk_ref, wv_ref, qb_ref, kb_ref, vb_ref, kf_ref, vf_ref, xb_ref, *, q_scale):
    @pl.when(pl.program_id(1) == 0)
    def _():
        xb_ref[...] = x_ref[...].astype(BF16)

    xb = xb_ref[...]
    qb_ref[...] = (_dot(xb, wq_ref[...]) * q_scale).astype(BF16)
    k = _dot(xb, wk_ref[...])
    kf_ref[...] = k
    kb_ref[...] = k.astype(BF16)
    v = _dot(xb, wv_ref[...])
    vf_ref[...] = v
    vb_ref[...] = v.astype(BF16)


def _qkv_proj(x, w, layer, q_scale, tm, tn):
    m, d = x.shape
    wa = w.shape[2] // 3
    nj = wa // tn
    row = lambda i, j: (i, j)
    out_bf = jax.ShapeDtypeStruct((m, wa), BF16)
    out_f = jax.ShapeDtypeStruct((m, wa), F32)
    return pl.pallas_call(
        functools.partial(_qkv_kernel, q_scale=q_scale),
        out_shape=(out_bf, out_bf, out_bf, out_f, out_f),
        grid=(m // tm, nj),
        in_specs=[
            pl.BlockSpec((tm, d), lambda i, j: (i, 0)),
            pl.BlockSpec((None, d, tn), lambda i, j: (layer, 0, j)),
            pl.BlockSpec((None, d, tn), lambda i, j: (layer, 0, nj + j)),
            pl.BlockSpec((None, d, tn), lambda i, j: (layer, 0, 2 * nj + j)),
        ],
        out_specs=[pl.BlockSpec((tm, tn), row)] * 5,
        scratch_shapes=[pltpu.VMEM((tm, d), BF16)],
        compiler_params=_params("arbitrary", "arbitrary"),
        name="qkv_proj",
    )(x, w, w, w)


def _mlstm_proj_kernel(x_ref, wq_ref, wk_ref, wv_ref, wo_ref, wg_ref, q_ref, k_ref, v_ref, o_ref, g_ref):
    xb = x_ref[...]
    q_ref[...] = _dot(xb, wq_ref[...]).astype(BF16)
    k_ref[...] = _dot(xb, wk_ref[...]).astype(BF16)
    v_ref[...] = _dot(xb, wv_ref[...]).astype(BF16)
    o_ref[...] = _dot(xb, wo_ref[...])

    @pl.when(pl.program_id(1) == 0)
    def _():
        g_ref[...] = _dot(xb, wg_ref[...])


def _mlstm_proj(xb, w, layer, w_gate, qk_w, v_w, tm, nj):
    m, d = xb.shape
    tq = qk_w // nj
    tv = v_w // nj
    return pl.pallas_call(
        _mlstm_proj_kernel,
        out_shape=(
            jax.ShapeDtypeStruct((m, qk_w), BF16),
            jax.ShapeDtypeStruct((m, qk_w), BF16),
            jax.ShapeDtypeStruct((m, v_w), BF16),
            jax.ShapeDtypeStruct((m, v_w), F32),
            jax.ShapeDtypeStruct((m, LANES), F32),
        ),
        grid=(m // tm, nj),
        in_specs=[
            pl.BlockSpec((tm, d), lambda i, j: (i, 0)),
            pl.BlockSpec((None, d, tq), lambda i, j: (layer, 0, j)),
            pl.BlockSpec((None, d, tq), lambda i, j: (layer, 0, nj + j)),
            pl.BlockSpec((None, d, tv), lambda i, j: (layer, 0, (2 * qk_w) // tv + j)),
            pl.BlockSpec((None, d, tv), lambda i, j: (layer, 0, (2 * qk_w + v_w) // tv + j)),
            pl.BlockSpec((d, LANES), lambda i, j: (0, 0)),
        ],
        out_specs=[
            pl.BlockSpec((tm, tq), lambda i, j: (i, j)),
            pl.BlockSpec((tm, tq), lambda i, j: (i, j)),
            pl.BlockSpec((tm, tv), lambda i, j: (i, j)),
            pl.BlockSpec((tm, tv), lambda i, j: (i, j)),
            pl.BlockSpec((tm, LANES), lambda i, j: (i, 0)),
        ],
        compiler_params=_params("arbitrary", "arbitrary"),
        name="mlstm_proj",
    )(xb, w, w, w, w, w_gate)


def _proj_ln_kernel(a_ref, w_ref, r_ref, g_ref, b_ref, of_ref, ob_ref, *, alpha):
    y = _dot(a_ref[...], w_ref[...])
    out = _layer_norm_rows(alpha * r_ref[...] + y, g_ref[...], b_ref[...])
    of_ref[...] = out
    ob_ref[...] = out.astype(BF16)


def _proj_ln(a, w, layer, resid, g, b, alpha, tm):
    m, kd = a.shape
    d = w.shape[2]
    return pl.pallas_call(
        functools.partial(_proj_ln_kernel, alpha=alpha),
        out_shape=(jax.ShapeDtypeStruct((m, d), F32), jax.ShapeDtypeStruct((m, d), BF16)),
        grid=(m // tm,),
        in_specs=[
            pl.BlockSpec((tm, kd), lambda i: (i, 0)),
            pl.BlockSpec((None, kd, d), lambda i: (layer, 0, 0), pipeline_mode=pl.Buffered(1)),
            pl.BlockSpec((tm, d), lambda i: (i, 0)),
            pl.BlockSpec((1, d), lambda i: (0, 0)),
            pl.BlockSpec((1, d), lambda i: (0, 0)),
        ],
        out_specs=[pl.BlockSpec((tm, d), lambda i: (i, 0))] * 2,
        compiler_params=_params("arbitrary"),
        name="proj_ln",
    )(a, w, resid, g, b)


def _silu_gate(cg, cv):
    return (cg * jax.nn.sigmoid(cg) * cv).astype(BF16)


def _ffn_up_prompt_kernel(x_ref, wg_ref, wv_ref, cwg_ref, cwv_ref, cbg_ref, cbv_ref,
                          o_ref, sg_ref, sv_ref, wgb_ref, wvb_ref, carry_ref, *, tiles_per_seq, rows):
    i = pl.program_id(1)
    tm = x_ref.shape[0]
    tn = o_ref.shape[1]

    @pl.when(i == 0)
    def _():
        wgb_ref[...] = wg_ref[...].astype(BF16)
        wvb_ref[...] = wv_ref[...].astype(BF16)

    @pl.when((i % tiles_per_seq) == 0)
    def _():
        carry_ref[...] = jnp.zeros_like(carry_ref)

    rowi = lax.broadcasted_iota(jnp.int32, (SUBLANES, tn), 0)

    def conv(u, prev, cw_ref, cb_ref):
        w0 = cw_ref[0:1, :]
        w1 = cw_ref[1:2, :]
        w2 = cw_ref[2:3, :]
        bias = cb_ref[...]
        c = bias + w0 * pltpu.roll(u, 2, 0) + w1 * pltpu.roll(u, 1, 0) + w2 * u
        u8 = u[:SUBLANES, :]
        p0 = prev[SUBLANES - 2:SUBLANES - 1, :]
        p1 = prev[SUBLANES - 1:SUBLANES, :]
        m1 = jnp.where(rowi == 0, p1, pltpu.roll(u8, 1, 0))
        m2 = jnp.where(rowi == 0, p0, jnp.where(rowi == 1, p1, pltpu.roll(u8, 2, 0)))
        c8 = bias + w0 * m2 + w1 * m1 + w2 * u8
        return jnp.concatenate([c8, c[SUBLANES:, :]], axis=0)

    prev_g = carry_ref[0]
    prev_v = carry_ref[1]
    for r in range(0, tm, rows):
        xb = x_ref[r:r + rows, :]
        ug = _dot(xb, wgb_ref[...])
        uv = _dot(xb, wvb_ref[...])
        o_ref[r:r + rows, :] = _silu_gate(conv(ug, prev_g, cwg_ref, cbg_ref), conv(uv, prev_v, cwv_ref, cbv_ref))
        prev_g = ug[rows - SUBLANES:, :]
        prev_v = uv[rows - SUBLANES:, :]
    carry_ref[0] = prev_g
    carry_ref[1] = prev_v
    sg_ref[...] = prev_g[SUBLANES - (CONV_W - 1):, :]
    sv_ref[...] = prev_v[SUBLANES - (CONV_W - 1):, :]


def _ffn_up_prompt(xb, w_up, layer, conv_w, conv_b, batch, tm, tn, rows):
    m, d = xb.shape
    f = w_up.shape[2] // 2
    nj = f // tn
    seq = m // batch
    tiles_per_seq = seq // tm
    state = jax.ShapeDtypeStruct((m // tm, CONV_W - 1, f), F32)
    g, sg, sv = pl.pallas_call(
        functools.partial(_ffn_up_prompt_kernel, tiles_per_seq=tiles_per_seq, rows=rows),
        out_shape=(jax.ShapeDtypeStruct((m, f), BF16), state, state),
        grid=(nj, m // tm),
        in_specs=[
            pl.BlockSpec((tm, d), lambda j, i: (i, 0)),
            pl.BlockSpec((None, d, tn), lambda j, i: (layer, 0, j)),
            pl.BlockSpec((None, d, tn), lambda j, i: (layer, 0, nj + j)),
            pl.BlockSpec((CONV_W, tn), lambda j, i: (0, j)),
            pl.BlockSpec((CONV_W, tn), lambda j, i: (0, nj + j)),
            pl.BlockSpec((1, tn), lambda j, i: (0, j)),
            pl.BlockSpec((1, tn), lambda j, i: (0, nj + j)),
        ],
        out_specs=[
            pl.BlockSpec((tm, tn), lambda j, i: (i, j)),
            pl.BlockSpec((None, CONV_W - 1, tn), lambda j, i: (i, 0, j)),
            pl.BlockSpec((None, CONV_W - 1, tn), lambda j, i: (i, 0, j)),
        ],
        scratch_shapes=[pltpu.VMEM((d, tn), BF16), pltpu.VMEM((d, tn), BF16),
                        pltpu.VMEM((2, SUBLANES, tn), F32)],
        compiler_params=_params("arbitrary", "arbitrary"),
        name="ffn_up_prompt",
    )(xb, w_up, w_up, conv_w, conv_w, conv_b, conv_b)
    last = slice(tiles_per_seq - 1, None, tiles_per_seq)
    return g, jnp.concatenate([sg[last], sv[last]], axis=-1)


def _ffn_up_sample_kernel(x_ref, wg_ref, wv_ref, cwg_ref, cwv_ref, cbg_ref, cbv_ref, pg_ref, pv_ref,
                          o_ref, sg_ref, sv_ref, *, n_t, nb):
    xb = x_ref[...]

    def conv(u, cw_ref, cb_ref, p_ref, s_ref):
        up = [p_ref[0], p_ref[1]] + [u[t * nb:(t + 1) * nb, :] for t in range(n_t)]
        s_ref[0] = up[n_t]
        s_ref[1] = up[n_t + 1]
        bias = cb_ref[...]
        return [bias + cw_ref[0:1, :] * up[t] + cw_ref[1:2, :] * up[t + 1] + cw_ref[2:3, :] * up[t + 2]
                for t in range(n_t)]

    cg = conv(_dot(xb, wg_ref[...].astype(BF16)), cwg_ref, cbg_ref, pg_ref, sg_ref)
    cv = conv(_dot(xb, wv_ref[...].astype(BF16)), cwv_ref, cbv_ref, pv_ref, sv_ref)
    for t in range(n_t):
        o_ref[t * nb:(t + 1) * nb, :] = _silu_gate(cg[t], cv[t])


def _ffn_up_sample(xb, w_up, layer, conv_w, conv_b, prefix, n_t, tn):
    m, d = xb.shape
    nb = m // n_t
    f = w_up.shape[2] // 2
    nj = f // tn
    state = jax.ShapeDtypeStruct((CONV_W - 1, nb, f), F32)
    g, sg, sv = pl.pallas_call(
        functools.partial(_ffn_up_sample_kernel, n_t=n_t, nb=nb),
        out_shape=(jax.ShapeDtypeStruct((m, f), BF16), state, state),
        grid=(nj,),
        in_specs=[
            pl.BlockSpec((m, d), lambda j: (0, 0)),
            pl.BlockSpec((None, d, tn), lambda j: (layer, 0, j)),
            pl.BlockSpec((None, d, tn), lambda j: (layer, 0, nj + j)),
            pl.BlockSpec((CONV_W, tn), lambda j: (0, j)),
            pl.BlockSpec((CONV_W, tn), lambda j: (0, nj + j)),
            pl.BlockSpec((1, tn), lambda j: (0, j)),
            pl.BlockSpec((1, tn), lambda j: (0, nj + j)),
            pl.BlockSpec((CONV_W - 1, nb, tn), lambda j: (0, 0, j)),
            pl.BlockSpec((CONV_W - 1, nb, tn), lambda j: (0, 0, nj + j)),
        ],
        out_specs=[
            pl.BlockSpec((m, tn), lambda j: (0, j)),
            pl.BlockSpec((CONV_W - 1, nb, tn), lambda j: (0, 0, j)),
            pl.BlockSpec((CONV_W - 1, nb, tn), lambda j: (0, 0, j)),
        ],
        compiler_params=_params("arbitrary"),
        name="ffn_up_sample",
    )(xb, w_up, w_up, conv_w, conv_w, conv_b, conv_b, prefix, prefix)
    return g, jnp.concatenate([sg, sv], axis=-1)


def _sb_prompt_kernel(bias_ref, q_ref, k_ref, v_ref, o_ref, acc_ref, run_ref, *, t, n_sub, unroll):
    h = pl.program_id(1)
    qi = pl.program_id(2)
    bias = bias_ref[h]
    row = lax.broadcasted_iota(jnp.int32, (t, t), 0)
    col = lax.broadcasted_iota(jnp.int32, (t, t), 1)
    suffix_incl = (row >= col).astype(BF16)
    valid = col < row

    def process(key_blocks, pairs):
        starts = [pl.multiple_of(j * t, t) for j in key_blocks]
        ks = [k_ref[pl.ds(st, t), :] for st in starts]
        vs = [v_ref[pl.ds(st, t), :] for st in starts]
        zs = [_dot_nt(q_ref[s * t:(s + 1) * t, :], ks[kb]) + bias for kb, s, _ in pairs]
        sps = [_softplus2(z) for z in zs]
        sps = [jnp.where(valid, sp, 0.0) if diagonal else sp for sp, (_, _, diagonal) in zip(sps, pairs)]
        incls = [_dot(sp.astype(BF16), suffix_incl) for sp in sps]
        runs = {s: run_ref[s] for s in sorted({s for _, s, _ in pairs})}
        probs = []
        for (_, s, diagonal), z, incl in zip(pairs, zs, incls):
            a = jnp.exp2(z - incl - runs[s])
            probs.append((jnp.where(valid, a, 0.0) if diagonal else a).astype(BF16))
            runs[s] = runs[s] + incl[:, 0:1]
        outs = [_dot(a, vs[kb]) for a, (kb, _, _) in zip(probs, pairs)]
        for s, run in runs.items():
            run_ref[s] = run
            acc_ref[s] += functools.reduce(lambda x, y: x + y, [o for o, (_, s2, _) in zip(outs, pairs) if s2 == s])

    top = [n_sub * qi + r for r in range(n_sub - 1, -1, -1)]
    acc_ref[...] = jnp.zeros_like(acc_ref)
    run_ref[...] = jnp.zeros_like(run_ref)
    process(top, [(slot, s, s == n_sub - 1 - slot) for slot in range(n_sub) for s in range(n_sub - 1 - slot, n_sub)])

    def body(it, carry):
        first = n_sub * qi - 1 - it * unroll
        process([first - u for u in range(unroll)], [(u, s, False) for u in range(unroll) for s in range(n_sub)])
        return carry

    lax.fori_loop(0, (n_sub * qi) // unroll, body, 0)
    for s in range(n_sub):
        o_ref[s * t:(s + 1) * t, :] = acc_ref[s].astype(BF16)


def _sb_prompt(q, k, v, bias, n_heads, t, n_sub):
    b, s, w = q.shape
    dh = w // n_heads
    tq = t * n_sub
    unroll = 2 if n_sub % 2 == 0 else 1
    return pl.pallas_call(
        functools.partial(_sb_prompt_kernel, t=t, n_sub=n_sub, unroll=unroll),
        out_shape=jax.ShapeDtypeStruct((b, s, w), BF16),
        grid=(b, n_heads, s // tq),
        in_specs=[
            pl.BlockSpec(memory_space=pltpu.SMEM),
            pl.BlockSpec((None, tq, dh), lambda bi, h, qi: (bi, qi, h)),
            pl.BlockSpec((None, s, dh), lambda bi, h, qi: (bi, 0, h)),
            pl.BlockSpec((None, s, dh), lambda bi, h, qi: (bi, 0, h)),
        ],
        out_specs=pl.BlockSpec((None, tq, dh), lambda bi, h, qi: (bi, qi, h)),
        scratch_shapes=[pltpu.VMEM((n_sub, t, dh), F32), pltpu.VMEM((n_sub, t, 1), F32)],
        compiler_params=_params("arbitrary", "arbitrary", "arbitrary"),
        name="sb_prompt",
    )(bias, q, k, v)


def _sb_sample_kernel(pt_ref, q_ref, bias_ref, kn_ref, vn_ref, spread_ref, same_head_ref, *refs,
                      n_heads, n_new, pages_per_step):
    kp_refs = refs[:pages_per_step]
    vp_refs = refs[pages_per_step:2 * pages_per_step]
    o_ref, acc_ref, run_ref, probs_ref = refs[2 * pages_per_step:]
    step = pl.program_id(1)
    page = kp_refs[0].shape[0] // n_heads
    q = q_ref[...]
    bias = bias_ref[...]
    ncol = q.shape[0]
    row = lax.broadcasted_iota(jnp.int32, (page, page), 0)
    col = lax.broadcasted_iota(jnp.int32, (page, page), 1)
    suffix_incl = (row >= col).astype(BF16)

    def head_rows(a, h):
        return a[h * SUBLANES:(h + 1) * SUBLANES]

    def weigh_head_major(a, val):
        cols = []
        for h in range(n_heads):
            parts = []
            if h > 0:
                parts.append(jnp.zeros((h * SUBLANES, page), F32))
            parts.append(head_rows(a, h))
            if h + 1 < n_heads:
                parts.append(jnp.zeros((ncol - (h + 1) * SUBLANES, page), F32))
            cols.append(jnp.concatenate(parts, axis=0))
        return _dot(jnp.concatenate(cols, axis=1).astype(BF16), val)

    def weigh_stored(a, val):
        spread = _dot(a, spread_ref[...]).astype(BF16) * same_head_ref[...]
        return _dot(spread, val)

    def weights(keys, valid, run):
        zt = _dot_nt(q, keys)
        z = jnp.concatenate([head_rows(zt, h)[:, h * page:(h + 1) * page] for h in range(n_heads)], axis=0) + bias
        sp = _softplus2(z)
        if valid is not None:
            sp = jnp.where(valid, sp, 0.0)
        incl = _split_dot(sp, suffix_incl, 2)
        a = jnp.exp2(z - incl - run)
        if valid is not None:
            a = jnp.where(valid, a, 0.0)
        return a, run + incl[:, 0:1]

    @pl.when(step == 0)
    def _():
        qry = lax.broadcasted_iota(jnp.int32, (ncol, page), 0) % SUBLANES
        key = lax.broadcasted_iota(jnp.int32, (ncol, page), 1)
        by_head = lambda r: jnp.concatenate([r[:, h * q.shape[1]:(h + 1) * q.shape[1]] for h in range(n_heads)], axis=0)
        a, run = weights(by_head(kn_ref[...]), (key < qry) & (qry < n_new), jnp.zeros((ncol, 1), F32))
        acc_ref[...] = weigh_head_major(a, by_head(vn_ref[...]))
        run_ref[...] = run

        @pl.when(pl.program_id(0) == 0)
        def _():
            probs_ref[...] = jnp.zeros_like(probs_ref)

    def keys_head_major(ref):
        heads = [ref[pl.ds(h, page, stride=n_heads), :] for h in range(n_heads)]
        return jnp.concatenate(heads, axis=0).astype(BF16)

    run = run_ref[...]
    total = jnp.zeros(acc_ref.shape, F32)
    for i in range(pages_per_step):
        total = total + weigh_stored(probs_ref[i], vp_refs[i][...].astype(BF16))
        a, run = weights(keys_head_major(kp_refs[i]), None, run)
        probs_ref[i] = a.astype(BF16)
    run_ref[...] = run
    acc_ref[...] += jnp.where(step > 0, total, 0.0)

    @pl.when(step == pl.num_programs(1) - 1)
    def _():
        o_ref[...] = acc_ref[...]


def _sb_sample(page_table, q, bias_cols, k_new, v_new, cache_k, cache_v, first_page, n_heads, n_new,
               pages_per_step):
    nb, n_pages = page_table.shape
    ncol, dh = q.shape[1], q.shape[2]
    w = n_heads * dh
    rows = cache_k.shape[1]
    page = rows // n_heads
    last = n_pages - 1

    n_steps = n_pages // pages_per_step

    def page_spec(i, lag):
        def index(b, p, pt):
            group = jnp.clip(p - lag, 0, n_steps - 1)
            return (pt[b, last - (group * pages_per_step + i)] + first_page, 0, 0)
        return pl.BlockSpec((None, rows, dh), index)

    c_tok = jnp.arange(rows)[None, :] // n_heads
    c_head = jnp.arange(rows)[None, :] % n_heads
    spread = (c_tok == jnp.arange(page)[:, None]).astype(BF16)
    same_head = (c_head == jnp.arange(ncol)[:, None] // SUBLANES).astype(BF16)
    const = lambda shape: pl.BlockSpec(shape, lambda b, p, pt: (0, 0))

    grid_spec = pltpu.PrefetchScalarGridSpec(
        num_scalar_prefetch=1,
        grid=(nb, n_steps + 1),
        in_specs=[
            pl.BlockSpec((None, ncol, dh), lambda b, p, pt: (b, 0, 0)),
            const((ncol, 1)),
            pl.BlockSpec((None, page, w), lambda b, p, pt: (b, 0, 0)),
            pl.BlockSpec((None, page, w), lambda b, p, pt: (b, 0, 0)),
            const((page, rows)),
            const((ncol, rows)),
        ] + [page_spec(i, 0) for i in range(pages_per_step)] + [page_spec(i, 1) for i in range(pages_per_step)],
        out_specs=pl.BlockSpec((None, ncol, dh), lambda b, p, pt: (b, 0, 0)),
        scratch_shapes=[pltpu.VMEM((ncol, dh), F32), pltpu.VMEM((ncol, 1), F32),
                        pltpu.VMEM((pages_per_step, ncol, page), BF16)],
    )
    return pl.pallas_call(
        functools.partial(_sb_sample_kernel, n_heads=n_heads, n_new=n_new,
                          pages_per_step=pages_per_step),
        out_shape=jax.ShapeDtypeStruct((nb, ncol, dh), F32),
        grid_spec=grid_spec,
        compiler_params=_params("arbitrary", "arbitrary"),
        name="sb_sample",
    )(page_table, q, bias_cols, k_new, v_new, spread, same_head,
      *([cache_k] * pages_per_step), *([cache_v] * pages_per_step))


def _mlstm_kernel(*refs, n_heads, n_valid, has_init):
    if has_init:
        (q_ref, k_ref, v_ref, o_ref, g_ref, bg_ref, gh_ref, c0_ref, n0_ref, m0_ref,
         y_ref, c_ref, n_ref, m_ref) = refs
    else:
        q_ref, k_ref, v_ref, o_ref, g_ref, bg_ref, gh_ref, y_ref, c_ref, n_ref, m_ref = refs
    chunk = pl.program_id(1)
    l = q_ref.shape[0]
    dk = q_ref.shape[1] // n_heads
    dv = v_ref.shape[1] // n_heads

    @pl.when(chunk == 0)
    def _():
        if has_init:
            c_ref[...] = c0_ref[...]
            n_ref[...] = n0_ref[...]
            m_ref[...] = m0_ref[...]
        else:
            c_ref[...] = jnp.zeros_like(c_ref)
            n_ref[...] = jnp.zeros_like(n_ref)
            m_ref[...] = jnp.zeros_like(m_ref)

    gates = g_ref[...] + bg_ref[...]
    lane = lax.broadcasted_iota(jnp.int32, gates.shape, 1)
    tok = lax.broadcasted_iota(jnp.int32, gates.shape, 0)
    is_forget = (lane >= n_heads) & (lane < 2 * n_heads)
    log_f = jnp.where(is_forget, -_softplus_log1p(-gates), 0.0)
    if n_valid < l:
        log_f = jnp.where(tok < n_valid, log_f, 0.0)
        gates = jnp.where(tok < n_valid, gates, -jnp.inf)
    ti = lax.broadcasted_iota(jnp.int32, (l, l), 0)
    si = lax.broadcasted_iota(jnp.int32, (l, l), 1)
    causal = si <= ti
    diag = si == ti
    tri = causal.astype(BF16)
    cum_col = _split_dot(log_f, tri, 3, left=True)

    def to_row(column):
        return jnp.sum(jnp.where(diag, column, 0.0), axis=0, keepdims=True)

    for h in range(n_heads):
        q = q_ref[:, h * dk:(h + 1) * dk]
        k = k_ref[:, h * dk:(h + 1) * dk]
        v = v_ref[:, h * dv:(h + 1) * dv]
        c = c_ref[h]
        n = n_ref[h:h + 1, :]
        m = m_ref[h:h + 1, 0:1]
        b_col = cum_col[:, n_heads + h:n_heads + h + 1]
        ig_col = gates[:, h:h + 1]
        b_row = to_row(b_col)
        ig_row = to_row(ig_col)

        d = jnp.where(causal, b_col - b_row + ig_row, -jnp.inf)
        inter = b_col + m
        m_t = jnp.maximum(inter, jnp.max(d, axis=1, keepdims=True))
        dw = jnp.exp(d - m_t)
        wi = jnp.exp(inter - m_t)
        s = _dot_nt(q, k) * (dw * dk ** -0.5)
        num = wi * _dot_nt(q, c.astype(BF16)) + _dot(s.astype(BF16), v)
        qn = jnp.sum(q.astype(F32) * n.astype(BF16).astype(F32), axis=1, keepdims=True)
        den = wi * qn + jnp.sum(s, axis=1, keepdims=True)
        hid = num / jnp.maximum(jnp.abs(den), jnp.exp(-m_t))

        b_last = b_col[l - 1:l, :]
        g_col = b_last - b_col + ig_col
        m_new = jnp.maximum(b_last + m, jnp.max(g_col, axis=0, keepdims=True))
        w_c = jnp.exp(b_last + m - m_new)
        w_s = jnp.exp(g_col - m_new)
        kw = w_s * (k.astype(F32) * dk ** -0.5)
        c_ref[h] = w_c * c + _dot_tn(v, kw.astype(BF16))
        n_ref[h:h + 1, :] = w_c * n + jnp.sum(kw, axis=0, keepdims=True)
        m_ref[h:h + 1, :] = jnp.broadcast_to(m_new, (1, m_ref.shape[1]))

        hid = hid * lax.rsqrt(jnp.mean(hid * hid, axis=1, keepdims=True) + HEAD_EPS)
        gate = jax.nn.sigmoid(o_ref[:, h * dv:(h + 1) * dv])
        y_ref[:, h * dv:(h + 1) * dv] = (hid * gh_ref[:, h * dv:(h + 1) * dv] * gate).astype(BF16)


def _mlstm(q, k, v, o, gates, b_gate, g_head, n_heads, l, n_valid, init=None):
    nb, t, qk_w = q.shape
    v_w = v.shape[2]
    dk = qk_w // n_heads
    dv = v_w // n_heads
    tok = lambda width: pl.BlockSpec((None, l, width), lambda b, c: (b, c, 0))
    const = lambda shape: pl.BlockSpec(shape, lambda b, c: (0,) * len(shape))
    c_spec = pl.BlockSpec((None, n_heads, dv, dk), lambda b, c: (b, 0, 0, 0))
    n_spec = pl.BlockSpec((None, n_heads, dk), lambda b, c: (b, 0, 0))
    m_spec = pl.BlockSpec((None, n_heads, LANES), lambda b, c: (b, 0, 0))
    in_specs = [tok(qk_w), tok(qk_w), tok(v_w), tok(v_w), tok(LANES), const((1, LANES)), const((1, v_w))]
    args = [q, k, v, o, gates, b_gate, g_head]
    if init is not None:
        in_specs += [c_spec, n_spec, m_spec]
        args += list(init)
    return pl.pallas_call(
        functools.partial(_mlstm_kernel, n_heads=n_heads, n_valid=n_valid, has_init=init is not None),
        out_shape=(
            jax.ShapeDtypeStruct((nb, t, v_w), BF16),
            jax.ShapeDtypeStruct((nb, n_heads, dv, dk), F32),
            jax.ShapeDtypeStruct((nb, n_heads, dk), F32),
            jax.ShapeDtypeStruct((nb, n_heads, LANES), F32),
        ),
        grid=(nb, t // l),
        in_specs=in_specs,
        out_specs=[tok(v_w), c_spec, n_spec, m_spec],
        compiler_params=_params("arbitrary", "arbitrary"),
        name="mlstm",
    )(*args)


def _pick(n, candidates):
    for c in candidates:
        if n % c == 0:
            return c
    return n


def kernel(x_prompt, x_sample, cache_k, cache_v, page_table, state_C, state_n, state_m, state_conv, w_in_a, w_out_a, sb_bias, w_in_b, b_gate_b, g_head_b, w_out_b, w_up, conv_w, conv_b, w_down, ln1_g, ln1_b, ln2_g, ln2_b):
    depth = w_up.shape[0]
    alpha = (2 * depth) ** 0.25
    bp, seq, d_model = x_prompt.shape
    bs, t_new, _ = x_sample.shape
    n_heads_a = sb_bias.shape[1]
    w_a = w_in_a.shape[2] // 3
    dh_a = w_a // n_heads_a
    n_heads_b = state_C.shape[2]
    dv_b, dk_b = state_C.shape[3], state_C.shape[4]
    qk_w, v_w = n_heads_b * dk_b, n_heads_b * dv_b
    d_ff = w_up.shape[2] // 2
    page = cache_k.shape[2]
    assert bs == SUBLANES and t_new <= SUBLANES and w_a == d_model

    mp = bp * seq
    ms = bs * t_new
    tm_p = _pick(seq, TM_PROMPT)
    tn_f = _pick(d_ff, TN_FF)

    xp = x_prompt.reshape(mp, d_model)
    xs = x_sample.transpose(1, 0, 2).reshape(ms, d_model)

    def to_batch_major(a, pad_to=None):
        a = a.reshape(t_new, bs, -1).transpose(1, 0, 2)
        if pad_to is not None:
            a = jnp.pad(a, ((0, 0), (0, pad_to - t_new), (0, 0)))
        return a

    def to_time_major(a):
        return a[:, :t_new].transpose(1, 0, 2).reshape(ms, -1)

    outs_p = dict(k=[], v=[], c=[], n=[], m=[], conv=[])
    outs_s = dict(k=[], v=[], c=[], n=[], m=[], conv=[])
    xp_f, xs_f = xp, xs
    xp_b = xs_b = None

    w_in_a_b, w_out_a_b = w_in_a.astype(BF16), w_out_a.astype(BF16)
    w_in_b_b, w_out_b_b = w_in_b.astype(BF16), w_out_b.astype(BF16)
    w_down_b = w_down.astype(BF16)
    q_scale = dh_a ** -0.5 * LOG2_E

    for i in range(depth):
        j = i // N_MIXERS
        if i % N_MIXERS == 0:
            w_out = w_out_a_b
            qb, kb, vb, kf, vf = _qkv_proj(xp_f, w_in_a_b, j, q_scale, tm_p, TN_QKV)
            outs_p["k"].append(kf.reshape(bp, seq, n_heads_a, dh_a))
            outs_p["v"].append(vf.reshape(bp, seq, n_heads_a, dh_a))
            shp = (bp, seq, w_a)
            t_attn = _pick(seq, T_ATTN)
            n_sub = _pick(seq // t_attn, ATTN_SUBTILES)
            bias2 = sb_bias[j] * LOG2_E
            mix_p = _sb_prompt(qb.reshape(shp), kb.reshape(shp), vb.reshape(shp), bias2, n_heads_a,
                               t_attn, n_sub).reshape(mp, w_a)
            qb, kb, vb, kf, vf = _qkv_proj(xs_f, w_in_a_b, j, q_scale, ms, TN_QKV)
            outs_s["k"].append(to_batch_major(kf).reshape(bs, t_new, n_heads_a, dh_a))
            outs_s["v"].append(to_batch_major(vf).reshape(bs, t_new, n_heads_a, dh_a))
            q8 = to_batch_major(qb, SUBLANES).reshape(bs, SUBLANES, n_heads_a, dh_a)
            q_hq = q8.transpose(0, 2, 1, 3).reshape(bs, n_heads_a * SUBLANES, dh_a)
            bias_cols = jnp.repeat(bias2, SUBLANES)[:, None]
            n_layers_a, pool = cache_k.shape[0], cache_k.shape[1]
            as_stored = lambda c: c.reshape(n_layers_a * pool, page * n_heads_a, dh_a)
            o_s = _sb_sample(page_table, q_hq, bias_cols, to_batch_major(kb, page), to_batch_major(vb, page),
                             as_stored(cache_k), as_stored(cache_v), j * pool, n_heads_a, t_new,
                             _pick(page_table.shape[1], PAGES_PER_STEP))
            o_s = o_s.reshape(bs, n_heads_a, SUBLANES, dh_a).transpose(0, 2, 1, 3).reshape(bs, SUBLANES, w_a)
            mix_s = to_time_major(o_s).astype(BF16)
        else:
            w_out = w_out_b_b
            n_main = 2 * qk_w + 2 * v_w
            w_gate = jnp.pad(w_in_b_b[j, :, n_main:], ((0, 0), (0, LANES - 2 * n_heads_b)))
            b_gate = jnp.pad(b_gate_b[j], (0, LANES - 2 * n_heads_b))[None, :]
            g_head = g_head_b[j][None, :]
            q, k, v, o, g = _mlstm_proj(xp_b, w_in_b_b, j, w_gate, qk_w, v_w, tm_p, 4)
            r3 = lambda a: a.reshape(bp, seq, -1)
            l = MLSTM_CHUNK if seq % MLSTM_CHUNK == 0 else seq
            mix_p, c, n, m = _mlstm(r3(q), r3(k), r3(v), r3(o), r3(g), b_gate, g_head, n_heads_b, l, l)
            mix_p = mix_p.reshape(mp, v_w)
            outs_p["c"].append(c)
            outs_p["n"].append(n)
            outs_p["m"].append(m[:, :, 0])
            q, k, v, o, g = _mlstm_proj(xs_b, w_in_b_b, j, w_gate, qk_w, v_w, ms, 4)
            ls = MLSTM_CHUNK if t_new % MLSTM_CHUNK == 0 else t_new
            assert ls == t_new and t_new <= MLSTM_CHUNK
            padl = lambda a: to_batch_major(a, MLSTM_CHUNK)
            m0 = jnp.broadcast_to(state_m[j][:, :, None], (bs, n_heads_b, LANES))
            y, c, n, m = _mlstm(padl(q), padl(k), padl(v), padl(o), padl(g), b_gate, g_head, n_heads_b,
                                MLSTM_CHUNK, t_new, init=(state_C[j], state_n[j], m0))
            mix_s = to_time_major(y)
            outs_s["c"].append(c)
            outs_s["n"].append(n)
            outs_s["m"].append(m[:, :, 0])

        g1, b1 = ln1_g[i][None, :], ln1_b[i][None, :]
        g2, b2 = ln2_g[i][None, :], ln2_b[i][None, :]
        cw, cb = conv_w[i], conv_b[i][None, :]

        xp_f, xp_b = _proj_ln(mix_p, w_out, j, xp_f, g1, b1, alpha, _pick(seq, TM_LN))
        gp, conv_p = _ffn_up_prompt(xp_b, w_up, i, cw, cb, bp, tm_p, tn_f, _pick(tm_p, FFN_ROWS))
        outs_p["conv"].append(conv_p)
        xp_f, xp_b = _proj_ln(gp, w_down_b, i, xp_f, g2, b2, alpha, _pick(seq, TM_DOWN))

        xs_f, xs_b = _proj_ln(mix_s, w_out, j, xs_f, g1, b1, alpha, ms)
        gs, conv_s = _ffn_up_sample(xs_b, w_up, i, cw, cb, state_conv[i].transpose(1, 0, 2), t_new, tn_f)
        outs_s["conv"].append(conv_s.transpose(1, 0, 2))
        xs_f, xs_b = _proj_ln(gs, w_down_b, i, xs_f, g2, b2, alpha, ms)

    y_prompt = xp_f.reshape(bp, seq, d_model)
    y_sample = xs_f.reshape(t_new, bs, d_model).transpose(1, 0, 2)
    st = jnp.stack
    return (y_prompt, y_sample, st(outs_p["k"]), st(outs_p["v"]), st(outs_s["k"]), st(outs_s["v"]),
            st(outs_p["c"]), st(outs_p["n"]), st(outs_p["m"]), st(outs_s["c"]), st(outs_s["n"]), st(outs_s["m"]),
            st(outs_p["conv"]), st(outs_s["conv"]))
```

```python
import functools

import jax
import jax.numpy as jnp
from jax import lax
from jax.experimental import pallas as pl
from jax.experimental.pallas import tpu as pltpu

F32 = jnp.float32
BF16 = jnp.bfloat16

LN_EPS = 1e-5
HEAD_EPS = 1e-6
CONV_W = 3
MLSTM_CHUNK = 128
N_MIXERS = 2
LOG2_E = 1.4426950408889634

SUBLANES = 8
LANES = 128
MXU_DIM = 256
VMEM_LIMIT_BYTES = 56 * 1024 * 1024

TM_PROMPT = (1024, 512, 256, 128)
TM_FFN = (2048, 1024, 512, 256, 128)
TM_LN = (512, 256, 128)
TM_DOWN = (256, 128)
TN_FF = (512, 256, 128)
TN_QKV = 512
T_ATTN = (MXU_DIM, LANES)
FFN_ROWS = (256, 128)
LN_ROWS = (256, 128)
ATTN_SUBTILES = (4, 2, 1)
PAGES_PER_STEP = (8, 4, 2, 1)


def _params(*semantics):
    return pltpu.CompilerParams(dimension_semantics=semantics, vmem_limit_bytes=VMEM_LIMIT_BYTES)


def _dot(a, b):
    return jnp.dot(a, b, preferred_element_type=F32)


def _dot_nt(a, b):
    return lax.dot_general(a, b, (((1,), (1,)), ((), ())), preferred_element_type=F32)


def _dot_tn(a, b):
    return lax.dot_general(a, b, (((0,), (0,)), ((), ())), preferred_element_type=F32)


def _split_dot(x, t, n_terms, left=False):
    acc = None
    rem = x
    for i in range(n_terms):
        part = rem.astype(BF16)
        term = _dot(t, part) if left else _dot(part, t)
        acc = term if acc is None else acc + term
        if i + 1 < n_terms:
            rem = rem - part.astype(F32)
    return acc


def _softplus2(z):
    return jnp.maximum(z, 0.0) + jnp.log2(1.0 + jnp.exp2(-jnp.abs(z)))


def _softplus_log1p(z):
    return jnp.maximum(z, 0.0) + jnp.log1p(jnp.exp(-jnp.abs(z)))


def _layer_norm_rows(z, g, b):
    mu = jnp.mean(z, axis=-1, keepdims=True)
    zc = z - mu
    var = jnp.mean(zc * zc, axis=-1, keepdims=True)
    return zc * lax.rsqrt(var + LN_EPS) * g + b


def _qkv_kernel(x_ref, wq_ref, wk_ref, wv_ref, qb_ref, kb_ref, vb_ref, kf_ref, vf_ref, xb_ref, *, q_scale):
    @pl.when(pl.program_id(1) == 0)
    def _():
        xb_ref[...] = x_ref[...].astype(BF16)

    xb = xb_ref[...]
    qb_ref[...] = (_dot(xb, wq_ref[...]) * q_scale).astype(BF16)
    k = _dot(xb, wk_ref[...])
    kf_ref[...] = k
    kb_ref[...] = k.astype(BF16)
    v = _dot(xb, wv_ref[...])
    vf_ref[...] = v
    vb_ref[...] = v.astype(BF16)


def _qkv_proj(x, w, layer, q_scale, tm, tn):
    m, d = x.shape
    wa = w.shape[2] // 3
    nj = wa // tn
    row = lambda i, j: (i, j)
    out_bf = jax.ShapeDtypeStruct((m, wa), BF16)
    out_f = jax.ShapeDtypeStruct((m, wa), F32)
    return pl.pallas_call(
        functools.partial(_qkv_kernel, q_scale=q_scale),
        out_shape=(out_bf, out_bf, out_bf, out_f, out_f),
        grid=(m // tm, nj),
        in_specs=[
            pl.BlockSpec((tm, d), lambda i, j: (i, 0)),
            pl.BlockSpec((None, d, tn), lambda i, j: (layer, 0, j)),
            pl.BlockSpec((None, d, tn), lambda i, j: (layer, 0, nj + j)),
            pl.BlockSpec((None, d, tn), lambda i, j: (layer, 0, 2 * nj + j)),
        ],
        out_specs=[pl.BlockSpec((tm, tn), row)] * 5,
        scratch_shapes=[pltpu.VMEM((tm, d), BF16)],
        compiler_params=_params("arbitrary", "arbitrary"),
        name="qkv_proj",
    )(x, w, w, w)


def _mlstm_proj_kernel(x_ref, wq_ref, wk_ref, wv_ref, wo_ref, wg_ref, q_ref, k_ref, v_ref, o_ref, g_ref):
    xb = x_ref[...]
    q_ref[...] = _dot(xb, wq_ref[...]).astype(BF16)
    k_ref[...] = _dot(xb, wk_ref[...]).astype(BF16)
    v_ref[...] = _dot(xb, wv_ref[...]).astype(BF16)
    o_ref[...] = _dot(xb, wo_ref[...])

    @pl.when(pl.program_id(1) == 0)
    def _():
        g_ref[...] = _dot(xb, wg_ref[...])


def _mlstm_proj(xb, w, layer, w_gate, qk_w, v_w, tm, nj):
    m, d = xb.shape
    tq = qk_w // nj
    tv = v_w // nj
    return pl.pallas_call(
        _mlstm_proj_kernel,
        out_shape=(
            jax.ShapeDtypeStruct((m, qk_w), BF16),
            jax.ShapeDtypeStruct((m, qk_w), BF16),
            jax.ShapeDtypeStruct((m, v_w), BF16),
            jax.ShapeDtypeStruct((m, v_w), F32),
            jax.ShapeDtypeStruct((m, LANES), F32),
        ),
        grid=(m // tm, nj),
        in_specs=[
            pl.BlockSpec((tm, d), lambda i, j: (i, 0)),
            pl.BlockSpec((None, d, tq), lambda i, j: (layer, 0, j)),
            pl.BlockSpec((None, d, tq), lambda i, j: (layer, 0, nj + j)),
            pl.BlockSpec((None, d, tv), lambda i, j: (layer, 0, (2 * qk_w) // tv + j)),
            pl.BlockSpec((None, d, tv), lambda i, j: (layer, 0, (2 * qk_w + v_w) // tv + j)),
            pl.BlockSpec((d, LANES), lambda i, j: (0, 0)),
        ],
        out_specs=[
            pl.BlockSpec((tm, tq), lambda i, j: (i, j)),
            pl.BlockSpec((tm, tq), lambda i, j: (i, j)),
            pl.BlockSpec((tm, tv), lambda i, j: (i, j)),
            pl.BlockSpec((tm, tv), lambda i, j: (i, j)),
            pl.BlockSpec((tm, LANES), lambda i, j: (i, 0)),
        ],
        compiler_params=_params("arbitrary", "arbitrary"),
        name="mlstm_proj",
    )(xb, w, w, w, w, w_gate)


def _proj_ln_kernel(a_ref, w_ref, r_ref, g_ref, b_ref, of_ref, ob_ref, *, alpha, rows):
    for r in range(0, a_ref.shape[0], rows):
        y = _dot(a_ref[r:r + rows, :], w_ref[...])
        out = _layer_norm_rows(alpha * r_ref[r:r + rows, :] + y, g_ref[...], b_ref[...])
        of_ref[r:r + rows, :] = out
        ob_ref[r:r + rows, :] = out.astype(BF16)


def _proj_ln(a, w, layer, resid, g, b, alpha, tm):
    m, kd = a.shape
    d = w.shape[2]
    return pl.pallas_call(
        functools.partial(_proj_ln_kernel, alpha=alpha, rows=_pick(tm, LN_ROWS)),
        out_shape=(jax.ShapeDtypeStruct((m, d), F32), jax.ShapeDtypeStruct((m, d), BF16)),
        grid=(m // tm,),
        in_specs=[
            pl.BlockSpec((tm, kd), lambda i: (i, 0)),
            pl.BlockSpec((None, kd, d), lambda i: (layer, 0, 0), pipeline_mode=pl.Buffered(1)),
            pl.BlockSpec((tm, d), lambda i: (i, 0)),
            pl.BlockSpec((1, d), lambda i: (0, 0)),
            pl.BlockSpec((1, d), lambda i: (0, 0)),
        ],
        out_specs=[pl.BlockSpec((tm, d), lambda i: (i, 0))] * 2,
        compiler_params=_params("arbitrary"),
        name="proj_ln",
    )(a, w, resid, g, b)


def _silu_gate(cg, cv):
    return (cg * jax.nn.sigmoid(cg) * cv).astype(BF16)


def _ffn_up_prompt_kernel(x_ref, wg_ref, wv_ref, cwg_ref, cwv_ref, cbg_ref, cbv_ref,
                          o_ref, sg_ref, sv_ref, wgb_ref, wvb_ref, carry_ref, *, tiles_per_seq, rows):
    i = pl.program_id(1)
    tm = x_ref.shape[0]
    tn = o_ref.shape[1]

    @pl.when(i == 0)
    def _():
        wgb_ref[...] = wg_ref[...].astype(BF16)
        wvb_ref[...] = wv_ref[...].astype(BF16)

    @pl.when((i % tiles_per_seq) == 0)
    def _():
        carry_ref[...] = jnp.zeros_like(carry_ref)

    rowi = lax.broadcasted_iota(jnp.int32, (SUBLANES, tn), 0)

    def conv(u, prev, cw_ref, cb_ref):
        w0 = cw_ref[0:1, :]
        w1 = cw_ref[1:2, :]
        w2 = cw_ref[2:3, :]
        bias = cb_ref[...]
        c = bias + w0 * pltpu.roll(u, 2, 0) + w1 * pltpu.roll(u, 1, 0) + w2 * u
        u8 = u[:SUBLANES, :]
        p0 = prev[SUBLANES - 2:SUBLANES - 1, :]
        p1 = prev[SUBLANES - 1:SUBLANES, :]
        m1 = jnp.where(rowi == 0, p1, pltpu.roll(u8, 1, 0))
        m2 = jnp.where(rowi == 0, p0, jnp.where(rowi == 1, p1, pltpu.roll(u8, 2, 0)))
        c8 = bias + w0 * m2 + w1 * m1 + w2 * u8
        return jnp.concatenate([c8, c[SUBLANES:, :]], axis=0)

    prev_g = carry_ref[0]
    prev_v = carry_ref[1]
    for r in range(0, tm, rows):
        xb = x_ref[r:r + rows, :]
        ug = _dot(xb, wgb_ref[...])
        uv = _dot(xb, wvb_ref[...])
        o_ref[r:r + rows, :] = _silu_gate(conv(ug, prev_g, cwg_ref, cbg_ref), conv(uv, prev_v, cwv_ref, cbv_ref))
        prev_g = ug[rows - SUBLANES:, :]
        prev_v = uv[rows - SUBLANES:, :]
    carry_ref[0] = prev_g
    carry_ref[1] = prev_v
    sg_ref[...] = prev_g[SUBLANES - (CONV_W - 1):, :]
    sv_ref[...] = prev_v[SUBLANES - (CONV_W - 1):, :]


def _ffn_up_prompt(xb, w_up, layer, conv_w, conv_b, batch, tm, tn, rows):
    m, d = xb.shape
    f = w_up.shape[2] // 2
    nj = f // tn
    seq = m // batch
    tiles_per_seq = seq // tm
    state = jax.ShapeDtypeStruct((m // tm, CONV_W - 1, f), F32)
    g, sg, sv = pl.pallas_call(
        functools.partial(_ffn_up_prompt_kernel, tiles_per_seq=tiles_per_seq, rows=rows),
        out_shape=(jax.ShapeDtypeStruct((m, f), BF16), state, state),
        grid=(nj, m // tm),
        in_specs=[
            pl.BlockSpec((tm, d), lambda j, i: (i, 0)),
            pl.BlockSpec((None, d, tn), lambda j, i: (layer, 0, j)),
            pl.BlockSpec((None, d, tn), lambda j, i: (layer, 0, nj + j)),
            pl.BlockSpec((CONV_W, tn), lambda j, i: (0, j)),
            pl.BlockSpec((CONV_W, tn), lambda j, i: (0, nj + j)),
            pl.BlockSpec((1, tn), lambda j, i: (0, j)),
            pl.BlockSpec((1, tn), lambda j, i: (0, nj + j)),
        ],
        out_specs=[
            pl.BlockSpec((tm, tn), lambda j, i: (i, j)),
            pl.BlockSpec((None, CONV_W - 1, tn), lambda j, i: (i, 0, j)),
            pl.BlockSpec((None, CONV_W - 1, tn), lambda j, i: (i, 0, j)),
        ],
        scratch_shapes=[pltpu.VMEM((d, tn), BF16), pltpu.VMEM((d, tn), BF16),
                        pltpu.VMEM((2, SUBLANES, tn), F32)],
        compiler_params=_params("arbitrary", "arbitrary"),
        name="ffn_up_prompt",
    )(xb, w_up, w_up, conv_w, conv_w, conv_b, conv_b)
    last = slice(tiles_per_seq - 1, None, tiles_per_seq)
    return g, jnp.concatenate([sg[last], sv[last]], axis=-1)


def _ffn_up_sample_kernel(x_ref, wg_ref, wv_ref, cwg_ref, cwv_ref, cbg_ref, cbv_ref, pg_ref, pv_ref,
                          o_ref, sg_ref, sv_ref, *, n_t, nb):
    xb = x_ref[...]

    def conv(u, cw_ref, cb_ref, p_ref, s_ref):
        up = [p_ref[0], p_ref[1]] + [u[t * nb:(t + 1) * nb, :] for t in range(n_t)]
        s_ref[0] = up[n_t]
        s_ref[1] = up[n_t + 1]
        bias = cb_ref[...]
        return [bias + cw_ref[0:1, :] * up[t] + cw_ref[1:2, :] * up[t + 1] + cw_ref[2:3, :] * up[t + 2]
                for t in range(n_t)]

    cg = conv(_dot(xb, wg_ref[...].astype(BF16)), cwg_ref, cbg_ref, pg_ref, sg_ref)
    cv = conv(_dot(xb, wv_ref[...].astype(BF16)), cwv_ref, cbv_ref, pv_ref, sv_ref)
    for t in range(n_t):
        o_ref[t * nb:(t + 1) * nb, :] = _silu_gate(cg[t], cv[t])


def _ffn_up_sample(xb, w_up, layer, conv_w, conv_b, prefix, n_t, tn):
    m, d = xb.shape
    nb = m // n_t
    f = w_up.shape[2] // 2
    nj = f // tn
    state = jax.ShapeDtypeStruct((CONV_W - 1, nb, f), F32)
    g, sg, sv = pl.pallas_call(
        functools.partial(_ffn_up_sample_kernel, n_t=n_t, nb=nb),
        out_shape=(jax.ShapeDtypeStruct((m, f), BF16), state, state),
        grid=(nj,),
        in_specs=[
            pl.BlockSpec((m, d), lambda j: (0, 0)),
            pl.BlockSpec((None, d, tn), lambda j: (layer, 0, j)),
            pl.BlockSpec((None, d, tn), lambda j: (layer, 0, nj + j)),
            pl.BlockSpec((CONV_W, tn), lambda j: (0, j)),
            pl.BlockSpec((CONV_W, tn), lambda j: (0, nj + j)),
            pl.BlockSpec((1, tn), lambda j: (0, j)),
            pl.BlockSpec((1, tn), lambda j: (0, nj + j)),
            pl.BlockSpec((CONV_W - 1, nb, tn), lambda j: (0, 0, j)),
            pl.BlockSpec((CONV_W - 1, nb, tn), lambda j: (0, 0, nj + j)),
        ],
        out_specs=[
            pl.BlockSpec((m, tn), lambda j: (0, j)),
            pl.BlockSpec((CONV_W - 1, nb, tn), lambda j: (0, 0, j)),
            pl.BlockSpec((CONV_W - 1, nb, tn), lambda j: (0, 0, j)),
        ],
        compiler_params=_params("arbitrary"),
        name="ffn_up_sample",
    )(xb, w_up, w_up, conv_w, conv_w, conv_b, conv_b, prefix, prefix)
    return g, jnp.concatenate([sg, sv], axis=-1)


def _sb_prompt_kernel(bias_ref, q_ref, k_ref, v_ref, o_ref, acc_ref, run_ref, *, t, n_sub, unroll):
    h = pl.program_id(1)
    qi = pl.program_id(2)
    bias = bias_ref[h]
    row = lax.broadcasted_iota(jnp.int32, (t, t), 0)
    col = lax.broadcasted_iota(jnp.int32, (t, t), 1)
    suffix_incl = (row >= col).astype(BF16)
    valid = col < row

    def process(key_blocks, pairs):
        starts = [pl.multiple_of(j * t, t) for j in key_blocks]
        ks = [k_ref[pl.ds(st, t), :] for st in starts]
        vs = [v_ref[pl.ds(st, t), :] for st in starts]
        zs = [_dot_nt(q_ref[s * t:(s + 1) * t, :], ks[kb]) + bias for kb, s, _ in pairs]
        sps = [_softplus2(z) for z in zs]
        sps = [jnp.where(valid, sp, 0.0) if diagonal else sp for sp, (_, _, diagonal) in zip(sps, pairs)]
        incls = [_dot(sp.astype(BF16), suffix_incl) for sp in sps]
        runs = {s: run_ref[s] for s in sorted({s for _, s, _ in pairs})}
        probs = []
        for (_, s, diagonal), z, incl in zip(pairs, zs, incls):
            a = jnp.exp2(z - incl - runs[s])
            probs.append((jnp.where(valid, a, 0.0) if diagonal else a).astype(BF16))
            runs[s] = runs[s] + incl[:, 0:1]
        outs = [_dot(a, vs[kb]) for a, (kb, _, _) in zip(probs, pairs)]
        for s, run in runs.items():
            run_ref[s] = run
            acc_ref[s] += functools.reduce(lambda x, y: x + y, [o for o, (_, s2, _) in zip(outs, pairs) if s2 == s])

    top = [n_sub * qi + r for r in range(n_sub - 1, -1, -1)]
    acc_ref[...] = jnp.zeros_like(acc_ref)
    run_ref[...] = jnp.zeros_like(run_ref)
    process(top, [(slot, s, s == n_sub - 1 - slot) for slot in range(n_sub) for s in range(n_sub - 1 - slot, n_sub)])

    def body(it, carry):
        first = n_sub * qi - 1 - it * unroll
        process([first - u for u in range(unroll)], [(u, s, False) for u in range(unroll) for s in range(n_sub)])
        return carry

    lax.fori_loop(0, (n_sub * qi) // unroll, body, 0)
    for s in range(n_sub):
        o_ref[s * t:(s + 1) * t, :] = acc_ref[s].astype(BF16)


def _sb_prompt(q, k, v, bias, n_heads, t, n_sub):
    b, s, w = q.shape
    dh = w // n_heads
    tq = t * n_sub
    unroll = 2 if n_sub % 2 == 0 else 1
    return pl.pallas_call(
        functools.partial(_sb_prompt_kernel, t=t, n_sub=n_sub, unroll=unroll),
        out_shape=jax.ShapeDtypeStruct((b, s, w), BF16),
        grid=(b, n_heads, s // tq),
        in_specs=[
            pl.BlockSpec(memory_space=pltpu.SMEM),
            pl.BlockSpec((None, tq, dh), lambda bi, h, qi: (bi, qi, h)),
            pl.BlockSpec((None, s, dh), lambda bi, h, qi: (bi, 0, h)),
            pl.BlockSpec((None, s, dh), lambda bi, h, qi: (bi, 0, h)),
        ],
        out_specs=pl.BlockSpec((None, tq, dh), lambda bi, h, qi: (bi, qi, h)),
        scratch_shapes=[pltpu.VMEM((n_sub, t, dh), F32), pltpu.VMEM((n_sub, t, 1), F32)],
        compiler_params=_params("arbitrary", "arbitrary", "arbitrary"),
        name="sb_prompt",
    )(bias, q, k, v)


def _sb_sample_kernel(pt_ref, q_ref, bias_ref, kn_ref, vn_ref, spread_ref, same_head_ref, *refs,
                      n_heads, n_new, pages_per_step):
    kp_refs = refs[:pages_per_step]
    vp_refs = refs[pages_per_step:2 * pages_per_step]
    o_ref, acc_ref, run_ref, probs_ref = refs[2 * pages_per_step:]
    step = pl.program_id(1)
    page = kp_refs[0].shape[0] // n_heads
    q = q_ref[...]
    bias = bias_ref[...]
    ncol = q.shape[0]
    row = lax.broadcasted_iota(jnp.int32, (page, page), 0)
    col = lax.broadcasted_iota(jnp.int32, (page, page), 1)
    suffix_incl = (row >= col).astype(BF16)

    def head_rows(a, h):
        return a[h * SUBLANES:(h + 1) * SUBLANES]

    def weigh_head_major(a, val):
        cols = []
        for h in range(n_heads):
            parts = []
            if h > 0:
                parts.append(jnp.zeros((h * SUBLANES, page), F32))
            parts.append(head_rows(a, h))
            if h + 1 < n_heads:
                parts.append(jnp.zeros((ncol - (h + 1) * SUBLANES, page), F32))
            cols.append(jnp.concatenate(parts, axis=0))
        return _dot(jnp.concatenate(cols, axis=1).astype(BF16), val)

    def weigh_stored(a, val):
        spread = _dot(a, spread_ref[...]).astype(BF16) * same_head_ref[...]
        return _dot(spread, val)

    def weights(keys, valid, run):
        zt = _dot_nt(q, keys)
        z = jnp.concatenate([head_rows(zt, h)[:, h * page:(h + 1) * page] for h in range(n_heads)], axis=0) + bias
        sp = _softplus2(z)
        if valid is not None:
            sp = jnp.where(valid, sp, 0.0)
        incl = _split_dot(sp, suffix_incl, 2)
        a = jnp.exp2(z - incl - run)
        if valid is not None:
            a = jnp.where(valid, a, 0.0)
        return a, run + incl[:, 0:1]

    @pl.when(step == 0)
    def _():
        qry = lax.broadcasted_iota(jnp.int32, (ncol, page), 0) % SUBLANES
        key = lax.broadcasted_iota(jnp.int32, (ncol, page), 1)
        by_head = lambda r: jnp.concatenate([r[:, h * q.shape[1]:(h + 1) * q.shape[1]] for h in range(n_heads)], axis=0)
        a, run = weights(by_head(kn_ref[...]), (key < qry) & (qry < n_new), jnp.zeros((ncol, 1), F32))
        acc_ref[...] = weigh_head_major(a, by_head(vn_ref[...]))
        run_ref[...] = run

        @pl.when(pl.program_id(0) == 0)
        def _():
            probs_ref[...] = jnp.zeros_like(probs_ref)

    def keys_head_major(ref):
        heads = [ref[pl.ds(h, page, stride=n_heads), :] for h in range(n_heads)]
        return jnp.concatenate(heads, axis=0).astype(BF16)

    run = run_ref[...]
    total = jnp.zeros(acc_ref.shape, F32)
    for i in range(pages_per_step):
        total = total + weigh_stored(probs_ref[i], vp_refs[i][...].astype(BF16))
        a, run = weights(keys_head_major(kp_refs[i]), None, run)
        probs_ref[i] = a.astype(BF16)
    run_ref[...] = run
    acc_ref[...] += jnp.where(step > 0, total, 0.0)

    @pl.when(step == pl.num_programs(1) - 1)
    def _():
        o_ref[...] = acc_ref[...]


def _sb_sample(page_table, q, bias_cols, k_new, v_new, cache_k, cache_v, first_page, n_heads, n_new,
               pages_per_step):
    nb, n_pages = page_table.shape
    ncol, dh = q.shape[1], q.shape[2]
    w = n_heads * dh
    rows = cache_k.shape[1]
    page = rows // n_heads
    last = n_pages - 1

    n_steps = n_pages // pages_per_step

    def page_spec(i, lag):
        def index(b, p, pt):
            group = jnp.clip(p - lag, 0, n_steps - 1)
            return (pt[b, last - (group * pages_per_step + i)] + first_page, 0, 0)
        return pl.BlockSpec((None, rows, dh), index)

    c_tok = jnp.arange(rows)[None, :] // n_heads
    c_head = jnp.arange(rows)[None, :] % n_heads
    spread = (c_tok == jnp.arange(page)[:, None]).astype(BF16)
    same_head = (c_head == jnp.arange(ncol)[:, None] // SUBLANES).astype(BF16)
    const = lambda shape: pl.BlockSpec(shape, lambda b, p, pt: (0, 0))

    grid_spec = pltpu.PrefetchScalarGridSpec(
        num_scalar_prefetch=1,
        grid=(nb, n_steps + 1),
        in_specs=[
            pl.BlockSpec((None, ncol, dh), lambda b, p, pt: (b, 0, 0)),
            const((ncol, 1)),
            pl.BlockSpec((None, page, w), lambda b, p, pt: (b, 0, 0)),
            pl.BlockSpec((None, page, w), lambda b, p, pt: (b, 0, 0)),
            const((page, rows)),
            const((ncol, rows)),
        ] + [page_spec(i, 0) for i in range(pages_per_step)] + [page_spec(i, 1) for i in range(pages_per_step)],
        out_specs=pl.BlockSpec((None, ncol, dh), lambda b, p, pt: (b, 0, 0)),
        scratch_shapes=[pltpu.VMEM((ncol, dh), F32), pltpu.VMEM((ncol, 1), F32),
                        pltpu.VMEM((pages_per_step, ncol, page), BF16)],
    )
    return pl.pallas_call(
        functools.partial(_sb_sample_kernel, n_heads=n_heads, n_new=n_new,
                          pages_per_step=pages_per_step),
        out_shape=jax.ShapeDtypeStruct((nb, ncol, dh), F32),
        grid_spec=grid_spec,
        compiler_params=_params("arbitrary", "arbitrary"),
        name="sb_sample",
    )(page_table, q, bias_cols, k_new, v_new, spread, same_head,
      *([cache_k] * pages_per_step), *([cache_v] * pages_per_step))


def _mlstm_kernel(*refs, n_heads, n_valid, has_init):
    if has_init:
        (q_ref, k_ref, v_ref, o_ref, g_ref, bg_ref, gh_ref, c0_ref, n0_ref, m0_ref,
         y_ref, c_ref, n_ref, m_ref) = refs
    else:
        q_ref, k_ref, v_ref, o_ref, g_ref, bg_ref, gh_ref, y_ref, c_ref, n_ref, m_ref = refs
    chunk = pl.program_id(1)
    l = q_ref.shape[0]
    dk = q_ref.shape[1] // n_heads
    dv = v_ref.shape[1] // n_heads

    @pl.when(chunk == 0)
    def _():
        if has_init:
            c_ref[...] = c0_ref[...]
            n_ref[...] = n0_ref[...]
            m_ref[...] = m0_ref[...]
        else:
            c_ref[...] = jnp.zeros_like(c_ref)
            n_ref[...] = jnp.zeros_like(n_ref)
            m_ref[...] = jnp.zeros_like(m_ref)

    gates = g_ref[...] + bg_ref[...]
    lane = lax.broadcasted_iota(jnp.int32, gates.shape, 1)
    tok = lax.broadcasted_iota(jnp.int32, gates.shape, 0)
    is_forget = (lane >= n_heads) & (lane < 2 * n_heads)
    log_f = jnp.where(is_forget, -_softplus_log1p(-gates), 0.0)
    if n_valid < l:
        log_f = jnp.where(tok < n_valid, log_f, 0.0)
        gates = jnp.where(tok < n_valid, gates, -jnp.inf)
    ti = lax.broadcasted_iota(jnp.int32, (l, l), 0)
    si = lax.broadcasted_iota(jnp.int32, (l, l), 1)
    causal = si <= ti
    diag = si == ti
    tri = causal.astype(BF16)
    cum_col = _split_dot(log_f, tri, 3, left=True)

    def to_row(column):
        return jnp.sum(jnp.where(diag, column, 0.0), axis=0, keepdims=True)

    heads = range(n_heads)
    qs = [q_ref[:, h * dk:(h + 1) * dk] for h in heads]
    ks = [k_ref[:, h * dk:(h + 1) * dk] for h in heads]
    vs = [v_ref[:, h * dv:(h + 1) * dv] for h in heads]
    cs = [c_ref[h] for h in heads]
    ns = [n_ref[h:h + 1, :] for h in heads]
    ms = [m_ref[h:h + 1, 0:1] for h in heads]
    b_cols = [cum_col[:, n_heads + h:n_heads + h + 1] for h in heads]
    ig_cols = [gates[:, h:h + 1] for h in heads]

    qk = [_dot_nt(q, k) for q, k in zip(qs, ks)]
    qc = [_dot_nt(q, c.astype(BF16)) for q, c in zip(qs, cs)]

    ss, wis, m_ts = [], [], []
    for h in heads:
        d = jnp.where(causal, b_cols[h] - to_row(b_cols[h]) + to_row(ig_cols[h]), -jnp.inf)
        inter = b_cols[h] + ms[h]
        m_t = jnp.maximum(inter, jnp.max(d, axis=1, keepdims=True))
        ss.append(qk[h] * (jnp.exp(d - m_t) * dk ** -0.5))
        wis.append(jnp.exp(inter - m_t))
        m_ts.append(m_t)

    kws, w_cs = [], []
    for h in heads:
        b_last = b_cols[h][l - 1:l, :]
        g_col = b_last - b_cols[h] + ig_cols[h]
        m_new = jnp.maximum(b_last + ms[h], jnp.max(g_col, axis=0, keepdims=True))
        w_cs.append(jnp.exp(b_last + ms[h] - m_new))
        kws.append(jnp.exp(g_col - m_new) * (ks[h].astype(F32) * dk ** -0.5))
        m_ref[h:h + 1, :] = jnp.broadcast_to(m_new, (1, m_ref.shape[1]))

    sv = [_dot(s.astype(BF16), v) for s, v in zip(ss, vs)]
    vk = [_dot_tn(v, kw.astype(BF16)) for v, kw in zip(vs, kws)]

    for h in heads:
        c_ref[h] = w_cs[h] * cs[h] + vk[h]
        n_ref[h:h + 1, :] = w_cs[h] * ns[h] + jnp.sum(kws[h], axis=0, keepdims=True)
        num = wis[h] * qc[h] + sv[h]
        qn = jnp.sum(qs[h].astype(F32) * ns[h].astype(BF16).astype(F32), axis=1, keepdims=True)
        den = wis[h] * qn + jnp.sum(ss[h], axis=1, keepdims=True)
        hid = num / jnp.maximum(jnp.abs(den), jnp.exp(-m_ts[h]))
        hid = hid * lax.rsqrt(jnp.mean(hid * hid, axis=1, keepdims=True) + HEAD_EPS)
        gate = jax.nn.sigmoid(o_ref[:, h * dv:(h + 1) * dv])
        y_ref[:, h * dv:(h + 1) * dv] = (hid * gh_ref[:, h * dv:(h + 1) * dv] * gate).astype(BF16)


def _mlstm(q, k, v, o, gates, b_gate, g_head, n_heads, l, n_valid, init=None):
    nb, t, qk_w = q.shape
    v_w = v.shape[2]
    dk = qk_w // n_heads
    dv = v_w // n_heads
    tok = lambda width: pl.BlockSpec((None, l, width), lambda b, c: (b, c, 0))
    const = lambda shape: pl.BlockSpec(shape, lambda b, c: (0,) * len(shape))
    c_spec = pl.BlockSpec((None, n_heads, dv, dk), lambda b, c: (b, 0, 0, 0))
    n_spec = pl.BlockSpec((None, n_heads, dk), lambda b, c: (b, 0, 0))
    m_spec = pl.BlockSpec((None, n_heads, LANES), lambda b, c: (b, 0, 0))
    in_specs = [tok(qk_w), tok(qk_w), tok(v_w), tok(v_w), tok(LANES), const((1, LANES)), const((1, v_w))]
    args = [q, k, v, o, gates, b_gate, g_head]
    if init is not None:
        in_specs += [c_spec, n_spec, m_spec]
        args += list(init)
    return pl.pallas_call(
        functools.partial(_mlstm_kernel, n_heads=n_heads, n_valid=n_valid, has_init=init is not None),
        out_shape=(
            jax.ShapeDtypeStruct((nb, t, v_w), BF16),
            jax.ShapeDtypeStruct((nb, n_heads, dv, dk), F32),
            jax.ShapeDtypeStruct((nb, n_heads, dk), F32),
            jax.ShapeDtypeStruct((nb, n_heads, LANES), F32),
        ),
        grid=(nb, t // l),
        in_specs=in_specs,
        out_specs=[tok(v_w), c_spec, n_spec, m_spec],
        compiler_params=_params("arbitrary", "arbitrary"),
        name="mlstm",
    )(*args)


def _pick(n, candidates):
    for c in candidates:
        if n % c == 0:
            return c
    return n


def kernel(x_prompt, x_sample, cache_k, cache_v, page_table, state_C, state_n, state_m, state_conv, w_in_a, w_out_a, sb_bias, w_in_b, b_gate_b, g_head_b, w_out_b, w_up, conv_w, conv_b, w_down, ln1_g, ln1_b, ln2_g, ln2_b):
    depth = w_up.shape[0]
    alpha = (2 * depth) ** 0.25
    bp, seq, d_model = x_prompt.shape
    bs, t_new, _ = x_sample.shape
    n_heads_a = sb_bias.shape[1]
    w_a = w_in_a.shape[2] // 3
    dh_a = w_a // n_heads_a
    n_heads_b = state_C.shape[2]
    dv_b, dk_b = state_C.shape[3], state_C.shape[4]
    qk_w, v_w = n_heads_b * dk_b, n_heads_b * dv_b
    d_ff = w_up.shape[2] // 2
    page = cache_k.shape[2]
    assert bs == SUBLANES and t_new <= SUBLANES and w_a == d_model

    mp = bp * seq
    ms = bs * t_new
    tm_p = _pick(seq, TM_PROMPT)
    tn_f = _pick(d_ff, TN_FF)

    xp = x_prompt.reshape(mp, d_model)
    xs = x_sample.transpose(1, 0, 2).reshape(ms, d_model)

    def to_batch_major(a, pad_to=None):
        a = a.reshape(t_new, bs, -1).transpose(1, 0, 2)
        if pad_to is not None:
            a = jnp.pad(a, ((0, 0), (0, pad_to - t_new), (0, 0)))
        return a

    def to_time_major(a):
        return a[:, :t_new].transpose(1, 0, 2).reshape(ms, -1)

    outs_p = dict(k=[], v=[], c=[], n=[], m=[], conv=[])
    outs_s = dict(k=[], v=[], c=[], n=[], m=[], conv=[])
    xp_f, xs_f = xp, xs
    xp_b = xs_b = None

    w_in_a_b, w_out_a_b = w_in_a.astype(BF16), w_out_a.astype(BF16)
    w_in_b_b, w_out_b_b = w_in_b.astype(BF16), w_out_b.astype(BF16)
    w_down_b = w_down.astype(BF16)
    q_scale = dh_a ** -0.5 * LOG2_E

    for i in range(depth):
        j = i // N_MIXERS
        if i % N_MIXERS == 0:
            w_out = w_out_a_b
            qb, kb, vb, kf, vf = _qkv_proj(xp_f, w_in_a_b, j, q_scale, tm_p, TN_QKV)
            outs_p["k"].append(kf.reshape(bp, seq, n_heads_a, dh_a))
            outs_p["v"].append(vf.reshape(bp, seq, n_heads_a, dh_a))
            shp = (bp, seq, w_a)
            t_attn = _pick(seq, T_ATTN)
            n_sub = _pick(seq // t_attn, ATTN_SUBTILES)
            bias2 = sb_bias[j] * LOG2_E
            mix_p = _sb_prompt(qb.reshape(shp), kb.reshape(shp), vb.reshape(shp), bias2, n_heads_a,
                               t_attn, n_sub).reshape(mp, w_a)
            qb, kb, vb, kf, vf = _qkv_proj(xs_f, w_in_a_b, j, q_scale, ms, TN_QKV)
            outs_s["k"].append(to_batch_major(kf).reshape(bs, t_new, n_heads_a, dh_a))
            outs_s["v"].append(to_batch_major(vf).reshape(bs, t_new, n_heads_a, dh_a))
            q8 = to_batch_major(qb, SUBLANES).reshape(bs, SUBLANES, n_heads_a, dh_a)
            q_hq = q8.transpose(0, 2, 1, 3).reshape(bs, n_heads_a * SUBLANES, dh_a)
            bias_cols = jnp.repeat(bias2, SUBLANES)[:, None]
            n_layers_a, pool = cache_k.shape[0], cache_k.shape[1]
            as_stored = lambda c: c.reshape(n_layers_a * pool, page * n_heads_a, dh_a)
            o_s = _sb_sample(page_table, q_hq, bias_cols, to_batch_major(kb, page), to_batch_major(vb, page),
                             as_stored(cache_k), as_stored(cache_v), j * pool, n_heads_a, t_new,
                             _pick(page_table.shape[1], PAGES_PER_STEP))
            o_s = o_s.reshape(bs, n_heads_a, SUBLANES, dh_a).transpose(0, 2, 1, 3).reshape(bs, SUBLANES, w_a)
            mix_s = to_time_major(o_s).astype(BF16)
        else:
            w_out = w_out_b_b
            n_main = 2 * qk_w + 2 * v_w
            w_gate = jnp.pad(w_in_b_b[j, :, n_main:], ((0, 0), (0, LANES - 2 * n_heads_b)))
            b_gate = jnp.pad(b_gate_b[j], (0, LANES - 2 * n_heads_b))[None, :]
            g_head = g_head_b[j][None, :]
            q, k, v, o, g = _mlstm_proj(xp_b, w_in_b_b, j, w_gate, qk_w, v_w, tm_p, 4)
            r3 = lambda a: a.reshape(bp, seq, -1)
            l = MLSTM_CHUNK if seq % MLSTM_CHUNK == 0 else seq
            mix_p, c, n, m = _mlstm(r3(q), r3(k), r3(v), r3(o), r3(g), b_gate, g_head, n_heads_b, l, l)
            mix_p = mix_p.reshape(mp, v_w)
            outs_p["c"].append(c)
            outs_p["n"].append(n)
            outs_p["m"].append(m[:, :, 0])
            q, k, v, o, g = _mlstm_proj(xs_b, w_in_b_b, j, w_gate, qk_w, v_w, ms, 4)
            ls = MLSTM_CHUNK if t_new % MLSTM_CHUNK == 0 else t_new
            assert ls == t_new and t_new <= MLSTM_CHUNK
            padl = lambda a: to_batch_major(a, MLSTM_CHUNK)
            m0 = jnp.broadcast_to(state_m[j][:, :, None], (bs, n_heads_b, LANES))
            y, c, n, m = _mlstm(padl(q), padl(k), padl(v), padl(o), padl(g), b_gate, g_head, n_heads_b,
                                MLSTM_CHUNK, t_new, init=(state_C[j], state_n[j], m0))
            mix_s = to_time_major(y)
            outs_s["c"].append(c)
            outs_s["n"].append(n)
            outs_s["m"].append(m[:, :, 0])

        g1, b1 = ln1_g[i][None, :], ln1_b[i][None, :]
        g2, b2 = ln2_g[i][None, :], ln2_b[i][None, :]
        cw, cb = conv_w[i], conv_b[i][None, :]

        xp_f, xp_b = _proj_ln(mix_p, w_out, j, xp_f, g1, b1, alpha, _pick(seq, TM_LN))
        tm_f = _pick(seq, TM_FFN)
        gp, conv_p = _ffn_up_prompt(xp_b, w_up, i, cw, cb, bp, tm_f, tn_f, _pick(tm_f, FFN_ROWS))
        outs_p["conv"].append(conv_p)
        xp_f, xp_b = _proj_ln(gp, w_down_b, i, xp_f, g2, b2, alpha, _pick(seq, TM_DOWN))

        xs_f, xs_b = _proj_ln(mix_s, w_out, j, xs_f, g1, b1, alpha, ms)
        gs, conv_s = _ffn_up_sample(xs_b, w_up, i, cw, cb, state_conv[i].transpose(1, 0, 2), t_new, tn_f)
        outs_s["conv"].append(conv_s.transpose(1, 0, 2))
        xs_f, xs_b = _proj_ln(gs, w_down_b, i, xs_f, g2, b2, alpha, ms)

    y_prompt = xp_f.reshape(bp, seq, d_model)
    y_sample = xs_f.reshape(t_new, bs, d_model).transpose(1, 0, 2)
    st = jnp.stack
    return (y_prompt, y_sample, st(outs_p["k"]), st(outs_p["v"]), st(outs_s["k"]), st(outs_s["v"]),
            st(outs_p["c"]), st(outs_p["n"]), st(outs_p["m"]), st(outs_s["c"]), st(outs_s["n"]), st(outs_s["m"]),
            st(outs_p["conv"]), st(outs_s["conv"]))
```

```python
import functools

import jax
import jax.numpy as jnp
from jax import lax
from jax.experimental import pallas as pl
from jax.experimental.pallas import tpu as pltpu

F32 = jnp.float32
BF16 = jnp.bfloat16

LN_EPS = 1e-5
HEAD_EPS = 1e-6
CONV_W = 3
MLSTM_CHUNK = 128
SAMPLE_CHUNK = 16
N_MIXERS = 2
LOG2_E = 1.4426950408889634

SUBLANES = 8
LANES = 128
MXU_DIM = 256
VMEM_LIMIT_BYTES = 56 * 1024 * 1024

TM_PROMPT = (1024, 512, 256, 128)
TM_FFN = (2048, 1024, 512, 256, 128)
TM_LN = (512, 256, 128)
TM_DOWN = (256, 128)
TN_FF = (512, 256, 128)
TN_QKV = 512
T_ATTN = (MXU_DIM, LANES)
FFN_ROWS = (256, 128)
LN_ROWS = (256, 128)
ATTN_SUBTILES = (4, 2, 1)
ATTN_UNROLL = (4, 2, 1)
PAGES_PER_STEP = (8, 4, 2, 1)


def _params(*semantics):
    return pltpu.CompilerParams(dimension_semantics=semantics, vmem_limit_bytes=VMEM_LIMIT_BYTES)


def _dot(a, b):
    return jnp.dot(a, b, preferred_element_type=F32)


def _dot_nt(a, b):
    return lax.dot_general(a, b, (((1,), (1,)), ((), ())), preferred_element_type=F32)


def _dot_tn(a, b):
    return lax.dot_general(a, b, (((0,), (0,)), ((), ())), preferred_element_type=F32)


def _split_dot(x, t, n_terms, left=False):
    acc = None
    rem = x
    for i in range(n_terms):
        part = rem.astype(BF16)
        term = _dot(t, part) if left else _dot(part, t)
        acc = term if acc is None else acc + term
        if i + 1 < n_terms:
            rem = rem - part.astype(F32)
    return acc


def _softplus2(z):
    return jnp.maximum(z, 0.0) + jnp.log2(1.0 + jnp.exp2(-jnp.abs(z)))


def _softplus_log1p(z):
    return jnp.maximum(z, 0.0) + jnp.log1p(jnp.exp(-jnp.abs(z)))


def _layer_norm_rows(z, g, b):
    mu = jnp.mean(z, axis=-1, keepdims=True)
    zc = z - mu
    var = jnp.mean(zc * zc, axis=-1, keepdims=True)
    return zc * lax.rsqrt(var + LN_EPS) * g + b


def _qkv_kernel(x_ref, wq_ref, wk_ref, wv_ref, qb_ref, kb_ref, vb_ref, kf_ref, vf_ref, xb_ref, *, q_scale):
    @pl.when(pl.program_id(1) == 0)
    def _():
        xb_ref[...] = x_ref[...].astype(BF16)

    xb = xb_ref[...]
    qb_ref[...] = (_dot(xb, wq_ref[...]) * q_scale).astype(BF16)
    k = _dot(xb, wk_ref[...])
    kf_ref[...] = k
    kb_ref[...] = k.astype(BF16)
    v = _dot(xb, wv_ref[...])
    vf_ref[...] = v
    vb_ref[...] = v.astype(BF16)


def _qkv_proj(x, w, layer, q_scale, tm, tn):
    m, d = x.shape
    wa = w.shape[2] // 3
    nj = wa // tn
    row = lambda i, j: (i, j)
    out_bf = jax.ShapeDtypeStruct((m, wa), BF16)
    out_f = jax.ShapeDtypeStruct((m, wa), F32)
    return pl.pallas_call(
        functools.partial(_qkv_kernel, q_scale=q_scale),
        out_shape=(out_bf, out_bf, out_bf, out_f, out_f),
        grid=(m // tm, nj),
        in_specs=[
            pl.BlockSpec((tm, d), lambda i, j: (i, 0)),
            pl.BlockSpec((None, d, tn), lambda i, j: (layer, 0, j)),
            pl.BlockSpec((None, d, tn), lambda i, j: (layer, 0, nj + j)),
            pl.BlockSpec((None, d, tn), lambda i, j: (layer, 0, 2 * nj + j)),
        ],
        out_specs=[pl.BlockSpec((tm, tn), row)] * 5,
        scratch_shapes=[pltpu.VMEM((tm, d), BF16)],
        compiler_params=_params("arbitrary", "arbitrary"),
        name="qkv_proj",
    )(x, w, w, w)


def _mlstm_proj_kernel(x_ref, wq_ref, wk_ref, wv_ref, wo_ref, wg_ref, q_ref, k_ref, v_ref, o_ref, g_ref):
    xb = x_ref[...]
    q_ref[...] = _dot(xb, wq_ref[...]).astype(BF16)
    k_ref[...] = _dot(xb, wk_ref[...]).astype(BF16)
    v_ref[...] = _dot(xb, wv_ref[...]).astype(BF16)
    o_ref[...] = _dot(xb, wo_ref[...])

    @pl.when(pl.program_id(1) == 0)
    def _():
        g_ref[...] = _dot(xb, wg_ref[...])


def _mlstm_proj(xb, w, layer, w_gate, qk_w, v_w, tm, nj):
    m, d = xb.shape
    tq = qk_w // nj
    tv = v_w // nj
    return pl.pallas_call(
        _mlstm_proj_kernel,
        out_shape=(
            jax.ShapeDtypeStruct((m, qk_w), BF16),
            jax.ShapeDtypeStruct((m, qk_w), BF16),
            jax.ShapeDtypeStruct((m, v_w), BF16),
            jax.ShapeDtypeStruct((m, v_w), F32),
            jax.ShapeDtypeStruct((m, LANES), F32),
        ),
        grid=(m // tm, nj),
        in_specs=[
            pl.BlockSpec((tm, d), lambda i, j: (i, 0)),
            pl.BlockSpec((None, d, tq), lambda i, j: (layer, 0, j)),
            pl.BlockSpec((None, d, tq), lambda i, j: (layer, 0, nj + j)),
            pl.BlockSpec((None, d, tv), lambda i, j: (layer, 0, (2 * qk_w) // tv + j)),
            pl.BlockSpec((None, d, tv), lambda i, j: (layer, 0, (2 * qk_w + v_w) // tv + j)),
            pl.BlockSpec((d, LANES), lambda i, j: (0, 0)),
        ],
        out_specs=[
            pl.BlockSpec((tm, tq), lambda i, j: (i, j)),
            pl.BlockSpec((tm, tq), lambda i, j: (i, j)),
            pl.BlockSpec((tm, tv), lambda i, j: (i, j)),
            pl.BlockSpec((tm, tv), lambda i, j: (i, j)),
            pl.BlockSpec((tm, LANES), lambda i, j: (i, 0)),
        ],
        compiler_params=_params("arbitrary", "arbitrary"),
        name="mlstm_proj",
    )(xb, w, w, w, w, w_gate)


def _proj_ln_kernel(a_ref, w_ref, r_ref, g_ref, b_ref, of_ref, ob_ref, *, alpha, rows):
    for r in range(0, a_ref.shape[0], rows):
        y = _dot(a_ref[r:r + rows, :], w_ref[...])
        out = _layer_norm_rows(alpha * r_ref[r:r + rows, :] + y, g_ref[...], b_ref[...])
        of_ref[r:r + rows, :] = out
        ob_ref[r:r + rows, :] = out.astype(BF16)


def _proj_ln(a, w, layer, resid, g, b, alpha, tm):
    m, kd = a.shape
    d = w.shape[2]
    return pl.pallas_call(
        functools.partial(_proj_ln_kernel, alpha=alpha, rows=_pick(tm, LN_ROWS)),
        out_shape=(jax.ShapeDtypeStruct((m, d), F32), jax.ShapeDtypeStruct((m, d), BF16)),
        grid=(m // tm,),
        in_specs=[
            pl.BlockSpec((tm, kd), lambda i: (i, 0)),
            pl.BlockSpec((None, kd, d), lambda i: (layer, 0, 0), pipeline_mode=pl.Buffered(1)),
            pl.BlockSpec((tm, d), lambda i: (i, 0)),
            pl.BlockSpec((1, d), lambda i: (0, 0)),
            pl.BlockSpec((1, d), lambda i: (0, 0)),
        ],
        out_specs=[pl.BlockSpec((tm, d), lambda i: (i, 0))] * 2,
        compiler_params=_params("arbitrary"),
        name="proj_ln",
    )(a, w, resid, g, b)


def _silu_gate(cg, cv):
    return (cg * jax.nn.sigmoid(cg) * cv).astype(BF16)


def _ffn_up_prompt_kernel(x_ref, wg_ref, wv_ref, cwg_ref, cwv_ref, cbg_ref, cbv_ref,
                          o_ref, sg_ref, sv_ref, wgb_ref, wvb_ref, carry_ref, *, tiles_per_seq, rows):
    i = pl.program_id(1)
    tm = x_ref.shape[0]
    tn = o_ref.shape[1]

    @pl.when(i == 0)
    def _():
        wgb_ref[...] = wg_ref[...].astype(BF16)
        wvb_ref[...] = wv_ref[...].astype(BF16)

    @pl.when((i % tiles_per_seq) == 0)
    def _():
        carry_ref[...] = jnp.zeros_like(carry_ref)

    rowi = lax.broadcasted_iota(jnp.int32, (SUBLANES, tn), 0)

    def conv(u, prev, cw_ref, cb_ref):
        w0 = cw_ref[0:1, :]
        w1 = cw_ref[1:2, :]
        w2 = cw_ref[2:3, :]
        bias = cb_ref[...]
        c = bias + w0 * pltpu.roll(u, 2, 0) + w1 * pltpu.roll(u, 1, 0) + w2 * u
        u8 = u[:SUBLANES, :]
        p0 = prev[SUBLANES - 2:SUBLANES - 1, :]
        p1 = prev[SUBLANES - 1:SUBLANES, :]
        m1 = jnp.where(rowi == 0, p1, pltpu.roll(u8, 1, 0))
        m2 = jnp.where(rowi == 0, p0, jnp.where(rowi == 1, p1, pltpu.roll(u8, 2, 0)))
        c8 = bias + w0 * m2 + w1 * m1 + w2 * u8
        return jnp.concatenate([c8, c[SUBLANES:, :]], axis=0)

    prev_g = carry_ref[0]
    prev_v = carry_ref[1]
    for r in range(0, tm, rows):
        xb = x_ref[r:r + rows, :]
        ug = _dot(xb, wgb_ref[...])
        uv = _dot(xb, wvb_ref[...])
        o_ref[r:r + rows, :] = _silu_gate(conv(ug, prev_g, cwg_ref, cbg_ref), conv(uv, prev_v, cwv_ref, cbv_ref))
        prev_g = ug[rows - SUBLANES:, :]
        prev_v = uv[rows - SUBLANES:, :]
    carry_ref[0] = prev_g
    carry_ref[1] = prev_v
    sg_ref[...] = prev_g[SUBLANES - (CONV_W - 1):, :]
    sv_ref[...] = prev_v[SUBLANES - (CONV_W - 1):, :]


def _ffn_up_prompt(xb, w_up, layer, conv_w, conv_b, batch, tm, tn, rows):
    m, d = xb.shape
    f = w_up.shape[2] // 2
    nj = f // tn
    seq = m // batch
    tiles_per_seq = seq // tm
    state = jax.ShapeDtypeStruct((m // tm, CONV_W - 1, f), F32)
    g, sg, sv = pl.pallas_call(
        functools.partial(_ffn_up_prompt_kernel, tiles_per_seq=tiles_per_seq, rows=rows),
        out_shape=(jax.ShapeDtypeStruct((m, f), BF16), state, state),
        grid=(nj, m // tm),
        in_specs=[
            pl.BlockSpec((tm, d), lambda j, i: (i, 0)),
            pl.BlockSpec((None, d, tn), lambda j, i: (layer, 0, j)),
            pl.BlockSpec((None, d, tn), lambda j, i: (layer, 0, nj + j)),
            pl.BlockSpec((CONV_W, tn), lambda j, i: (0, j)),
            pl.BlockSpec((CONV_W, tn), lambda j, i: (0, nj + j)),
            pl.BlockSpec((1, tn), lambda j, i: (0, j)),
            pl.BlockSpec((1, tn), lambda j, i: (0, nj + j)),
        ],
        out_specs=[
            pl.BlockSpec((tm, tn), lambda j, i: (i, j)),
            pl.BlockSpec((None, CONV_W - 1, tn), lambda j, i: (i, 0, j)),
            pl.BlockSpec((None, CONV_W - 1, tn), lambda j, i: (i, 0, j)),
        ],
        scratch_shapes=[pltpu.VMEM((d, tn), BF16), pltpu.VMEM((d, tn), BF16),
                        pltpu.VMEM((2, SUBLANES, tn), F32)],
        compiler_params=_params("arbitrary", "arbitrary"),
        name="ffn_up_prompt",
    )(xb, w_up, w_up, conv_w, conv_w, conv_b, conv_b)
    last = slice(tiles_per_seq - 1, None, tiles_per_seq)
    return g, jnp.concatenate([sg[last], sv[last]], axis=-1)


def _ffn_up_sample_kernel(x_ref, wg_ref, wv_ref, cwg_ref, cwv_ref, cbg_ref, cbv_ref, pg_ref, pv_ref,
                          o_ref, sg_ref, sv_ref, *, n_t, nb):
    xb = x_ref[...]

    def conv(u, cw_ref, cb_ref, p_ref, s_ref):
        up = [p_ref[0], p_ref[1]] + [u[t * nb:(t + 1) * nb, :] for t in range(n_t)]
        s_ref[0] = up[n_t]
        s_ref[1] = up[n_t + 1]
        bias = cb_ref[...]
        return [bias + cw_ref[0:1, :] * up[t] + cw_ref[1:2, :] * up[t + 1] + cw_ref[2:3, :] * up[t + 2]
                for t in range(n_t)]

    cg = conv(_dot(xb, wg_ref[...].astype(BF16)), cwg_ref, cbg_ref, pg_ref, sg_ref)
    cv = conv(_dot(xb, wv_ref[...].astype(BF16)), cwv_ref, cbv_ref, pv_ref, sv_ref)
    for t in range(n_t):
        o_ref[t * nb:(t + 1) * nb, :] = _silu_gate(cg[t], cv[t])


def _ffn_up_sample(xb, w_up, layer, conv_w, conv_b, prefix, n_t, tn):
    m, d = xb.shape
    nb = m // n_t
    f = w_up.shape[2] // 2
    nj = f // tn
    state = jax.ShapeDtypeStruct((CONV_W - 1, nb, f), F32)
    g, sg, sv = pl.pallas_call(
        functools.partial(_ffn_up_sample_kernel, n_t=n_t, nb=nb),
        out_shape=(jax.ShapeDtypeStruct((m, f), BF16), state, state),
        grid=(nj,),
        in_specs=[
            pl.BlockSpec((m, d), lambda j: (0, 0)),
            pl.BlockSpec((None, d, tn), lambda j: (layer, 0, j)),
            pl.BlockSpec((None, d, tn), lambda j: (layer, 0, nj + j)),
            pl.BlockSpec((CONV_W, tn), lambda j: (0, j)),
            pl.BlockSpec((CONV_W, tn), lambda j: (0, nj + j)),
            pl.BlockSpec((1, tn), lambda j: (0, j)),
            pl.BlockSpec((1, tn), lambda j: (0, nj + j)),
            pl.BlockSpec((CONV_W - 1, nb, tn), lambda j: (0, 0, j)),
            pl.BlockSpec((CONV_W - 1, nb, tn), lambda j: (0, 0, nj + j)),
        ],
        out_specs=[
            pl.BlockSpec((m, tn), lambda j: (0, j)),
            pl.BlockSpec((CONV_W - 1, nb, tn), lambda j: (0, 0, j)),
            pl.BlockSpec((CONV_W - 1, nb, tn), lambda j: (0, 0, j)),
        ],
        compiler_params=_params("arbitrary"),
        name="ffn_up_sample",
    )(xb, w_up, w_up, conv_w, conv_w, conv_b, conv_b, prefix, prefix)
    return g, jnp.concatenate([sg, sv], axis=-1)


def _sb_prompt_kernel(bias_ref, q_ref, k_ref, v_ref, o_ref, acc_ref, run_ref, *, t, n_sub, unroll):
    h = pl.program_id(1)
    qi = pl.program_id(2)
    bias = bias_ref[h]
    row = lax.broadcasted_iota(jnp.int32, (t, t), 0)
    col = lax.broadcasted_iota(jnp.int32, (t, t), 1)
    suffix_incl = (row >= col).astype(BF16)
    valid = col < row

    def process(key_blocks, pairs):
        starts = [pl.multiple_of(j * t, t) for j in key_blocks]
        ks = [k_ref[pl.ds(st, t), :] for st in starts]
        vs = [v_ref[pl.ds(st, t), :] for st in starts]
        zs = [_dot_nt(q_ref[s * t:(s + 1) * t, :], ks[kb]) + bias for kb, s, _ in pairs]
        sps = [_softplus2(z) for z in zs]
        sps = [jnp.where(valid, sp, 0.0) if diagonal else sp for sp, (_, _, diagonal) in zip(sps, pairs)]
        incls = [_dot(sp.astype(BF16), suffix_incl) for sp in sps]
        runs = {s: run_ref[s] for s in sorted({s for _, s, _ in pairs})}
        probs = []
        for (_, s, diagonal), z, incl in zip(pairs, zs, incls):
            a = jnp.exp2(z - incl - runs[s])
            probs.append((jnp.where(valid, a, 0.0) if diagonal else a).astype(BF16))
            runs[s] = runs[s] + incl[:, 0:1]
        outs = [_dot(a, vs[kb]) for a, (kb, _, _) in zip(probs, pairs)]
        for s, run in runs.items():
            run_ref[s] = run
            acc_ref[s] += functools.reduce(lambda x, y: x + y, [o for o, (_, s2, _) in zip(outs, pairs) if s2 == s])

    top = [n_sub * qi + r for r in range(n_sub - 1, -1, -1)]
    acc_ref[...] = jnp.zeros_like(acc_ref)
    run_ref[...] = jnp.zeros_like(run_ref)
    process(top, [(slot, s, s == n_sub - 1 - slot) for slot in range(n_sub) for s in range(n_sub - 1 - slot, n_sub)])

    def body(it, carry):
        first = n_sub * qi - 1 - it * unroll
        process([first - u for u in range(unroll)], [(u, s, False) for u in range(unroll) for s in range(n_sub)])
        return carry

    lax.fori_loop(0, (n_sub * qi) // unroll, body, 0)
    for s in range(n_sub):
        o_ref[s * t:(s + 1) * t, :] = acc_ref[s].astype(BF16)


def _sb_prompt(q, k, v, bias, n_heads, t, n_sub):
    b, s, w = q.shape
    dh = w // n_heads
    tq = t * n_sub
    unroll = _pick(n_sub, ATTN_UNROLL)
    return pl.pallas_call(
        functools.partial(_sb_prompt_kernel, t=t, n_sub=n_sub, unroll=unroll),
        out_shape=jax.ShapeDtypeStruct((b, s, w), BF16),
        grid=(b, n_heads, s // tq),
        in_specs=[
            pl.BlockSpec(memory_space=pltpu.SMEM),
            pl.BlockSpec((None, tq, dh), lambda bi, h, qi: (bi, qi, h)),
            pl.BlockSpec((None, s, dh), lambda bi, h, qi: (bi, 0, h)),
            pl.BlockSpec((None, s, dh), lambda bi, h, qi: (bi, 0, h)),
        ],
        out_specs=pl.BlockSpec((None, tq, dh), lambda bi, h, qi: (bi, qi, h)),
        scratch_shapes=[pltpu.VMEM((n_sub, t, dh), F32), pltpu.VMEM((n_sub, t, 1), F32)],
        compiler_params=_params("arbitrary", "arbitrary", "arbitrary"),
        name="sb_prompt",
    )(bias, q, k, v)


def _sb_sample_kernel(pt_ref, q_ref, bias_ref, kn_ref, vn_ref, spread_ref, same_head_ref, *refs,
                      n_heads, n_new, pages_per_step):
    kp_refs = refs[:pages_per_step]
    vp_refs = refs[pages_per_step:2 * pages_per_step]
    o_ref, acc_ref, run_ref, probs_ref = refs[2 * pages_per_step:]
    step = pl.program_id(1)
    page = kp_refs[0].shape[0] // n_heads
    q = q_ref[...]
    bias = bias_ref[...]
    ncol = q.shape[0]
    row = lax.broadcasted_iota(jnp.int32, (page, page), 0)
    col = lax.broadcasted_iota(jnp.int32, (page, page), 1)
    suffix_incl = (row >= col).astype(BF16)

    def head_rows(a, h):
        return a[h * SUBLANES:(h + 1) * SUBLANES]

    def weigh_head_major(a, val):
        cols = []
        for h in range(n_heads):
            parts = []
            if h > 0:
                parts.append(jnp.zeros((h * SUBLANES, page), F32))
            parts.append(head_rows(a, h))
            if h + 1 < n_heads:
                parts.append(jnp.zeros((ncol - (h + 1) * SUBLANES, page), F32))
            cols.append(jnp.concatenate(parts, axis=0))
        return _dot(jnp.concatenate(cols, axis=1).astype(BF16), val)

    def weigh_stored(a, val):
        spread = _dot(a, spread_ref[...]).astype(BF16) * same_head_ref[...]
        return _dot(spread, val)

    def weights(keys, valid, run):
        zt = _dot_nt(q, keys)
        z = jnp.concatenate([head_rows(zt, h)[:, h * page:(h + 1) * page] for h in range(n_heads)], axis=0) + bias
        sp = _softplus2(z)
        if valid is not None:
            sp = jnp.where(valid, sp, 0.0)
        incl = _split_dot(sp, suffix_incl, 2)
        a = jnp.exp2(z - incl - run)
        if valid is not None:
            a = jnp.where(valid, a, 0.0)
        return a, run + incl[:, 0:1]

    @pl.when(step == 0)
    def _():
        qry = lax.broadcasted_iota(jnp.int32, (ncol, page), 0) % SUBLANES
        key = lax.broadcasted_iota(jnp.int32, (ncol, page), 1)
        by_head = lambda r: jnp.concatenate([r[:, h * q.shape[1]:(h + 1) * q.shape[1]] for h in range(n_heads)], axis=0)
        a, run = weights(by_head(kn_ref[...]), (key < qry) & (qry < n_new), jnp.zeros((ncol, 1), F32))
        acc_ref[...] = weigh_head_major(a, by_head(vn_ref[...]))
        run_ref[...] = run

        @pl.when(pl.program_id(0) == 0)
        def _():
            probs_ref[...] = jnp.zeros_like(probs_ref)

    def keys_head_major(ref):
        heads = [ref[pl.ds(h, page, stride=n_heads), :] for h in range(n_heads)]
        return jnp.concatenate(heads, axis=0).astype(BF16)

    run = run_ref[...]
    total = jnp.zeros(acc_ref.shape, F32)
    for i in range(pages_per_step):
        total = total + weigh_stored(probs_ref[i], vp_refs[i][...].astype(BF16))
        a, run = weights(keys_head_major(kp_refs[i]), None, run)
        probs_ref[i] = a.astype(BF16)
    run_ref[...] = run
    acc_ref[...] += jnp.where(step > 0, total, 0.0)

    @pl.when(step == pl.num_programs(1) - 1)
    def _():
        o_ref[...] = acc_ref[...]


def _sb_sample(page_table, q, bias_cols, k_new, v_new, cache_k, cache_v, first_page, n_heads, n_new,
               pages_per_step):
    nb, n_pages = page_table.shape
    ncol, dh = q.shape[1], q.shape[2]
    w = n_heads * dh
    rows = cache_k.shape[1]
    page = rows // n_heads
    last = n_pages - 1

    n_steps = n_pages // pages_per_step

    def page_spec(i, lag):
        def index(b, p, pt):
            group = jnp.clip(p - lag, 0, n_steps - 1)
            return (pt[b, last - (group * pages_per_step + i)] + first_page, 0, 0)
        return pl.BlockSpec((None, rows, dh), index)

    c_tok = jnp.arange(rows)[None, :] // n_heads
    c_head = jnp.arange(rows)[None, :] % n_heads
    spread = (c_tok == jnp.arange(page)[:, None]).astype(BF16)
    same_head = (c_head == jnp.arange(ncol)[:, None] // SUBLANES).astype(BF16)
    const = lambda shape: pl.BlockSpec(shape, lambda b, p, pt: (0, 0))

    grid_spec = pltpu.PrefetchScalarGridSpec(
        num_scalar_prefetch=1,
        grid=(nb, n_steps + 1),
        in_specs=[
            pl.BlockSpec((None, ncol, dh), lambda b, p, pt: (b, 0, 0)),
            const((ncol, 1)),
            pl.BlockSpec((None, page, w), lambda b, p, pt: (b, 0, 0)),
            pl.BlockSpec((None, page, w), lambda b, p, pt: (b, 0, 0)),
            const((page, rows)),
            const((ncol, rows)),
        ] + [page_spec(i, 0) for i in range(pages_per_step)] + [page_spec(i, 1) for i in range(pages_per_step)],
        out_specs=pl.BlockSpec((None, ncol, dh), lambda b, p, pt: (b, 0, 0)),
        scratch_shapes=[pltpu.VMEM((ncol, dh), F32), pltpu.VMEM((ncol, 1), F32),
                        pltpu.VMEM((pages_per_step, ncol, page), BF16)],
    )
    return pl.pallas_call(
        functools.partial(_sb_sample_kernel, n_heads=n_heads, n_new=n_new,
                          pages_per_step=pages_per_step),
        out_shape=jax.ShapeDtypeStruct((nb, ncol, dh), F32),
        grid_spec=grid_spec,
        compiler_params=_params("arbitrary", "arbitrary"),
        name="sb_sample",
    )(page_table, q, bias_cols, k_new, v_new, spread, same_head,
      *([cache_k] * pages_per_step), *([cache_v] * pages_per_step))


def _mlstm_kernel(*refs, n_heads, n_valid, has_init):
    if has_init:
        (q_ref, k_ref, v_ref, o_ref, g_ref, bg_ref, gh_ref, c0_ref, n0_ref, m0_ref,
         y_ref, c_ref, n_ref, m_ref) = refs
    else:
        q_ref, k_ref, v_ref, o_ref, g_ref, bg_ref, gh_ref, y_ref, c_ref, n_ref, m_ref = refs
    chunk = pl.program_id(1)
    l = q_ref.shape[0]
    dk = q_ref.shape[1] // n_heads
    dv = v_ref.shape[1] // n_heads

    @pl.when(chunk == 0)
    def _():
        if has_init:
            c_ref[...] = c0_ref[...]
            n_ref[...] = n0_ref[...]
            m_ref[...] = m0_ref[...]
        else:
            c_ref[...] = jnp.zeros_like(c_ref)
            n_ref[...] = jnp.zeros_like(n_ref)
            m_ref[...] = jnp.zeros_like(m_ref)

    gates = g_ref[...] + bg_ref[...]
    lane = lax.broadcasted_iota(jnp.int32, gates.shape, 1)
    tok = lax.broadcasted_iota(jnp.int32, gates.shape, 0)
    is_forget = (lane >= n_heads) & (lane < 2 * n_heads)
    log_f = jnp.where(is_forget, -_softplus_log1p(-gates), 0.0)
    if n_valid < l:
        log_f = jnp.where(tok < n_valid, log_f, 0.0)
        gates = jnp.where(tok < n_valid, gates, -jnp.inf)
    ti = lax.broadcasted_iota(jnp.int32, (l, l), 0)
    si = lax.broadcasted_iota(jnp.int32, (l, l), 1)
    causal = si <= ti
    diag = si == ti
    tri = causal.astype(BF16)
    cum_col = _split_dot(log_f, tri, 3, left=True)

    def to_row(column):
        return jnp.sum(jnp.where(diag, column, 0.0), axis=0, keepdims=True)

    heads = range(n_heads)
    qs = [q_ref[:, h * dk:(h + 1) * dk] for h in heads]
    ks = [k_ref[:, h * dk:(h + 1) * dk] for h in heads]
    vs = [v_ref[:, h * dv:(h + 1) * dv] for h in heads]
    cs = [c_ref[h] for h in heads]
    ns = [n_ref[h:h + 1, :] for h in heads]
    ms = [m_ref[h:h + 1, 0:1] for h in heads]
    b_cols = [cum_col[:, n_heads + h:n_heads + h + 1] for h in heads]
    ig_cols = [gates[:, h:h + 1] for h in heads]

    qk = [_dot_nt(q, k) for q, k in zip(qs, ks)]
    qc = [_dot_nt(q, c.astype(BF16)) for q, c in zip(qs, cs)]

    ss, wis, m_ts = [], [], []
    for h in heads:
        d = jnp.where(causal, b_cols[h] - to_row(b_cols[h]) + to_row(ig_cols[h]), -jnp.inf)
        inter = b_cols[h] + ms[h]
        m_t = jnp.maximum(inter, jnp.max(d, axis=1, keepdims=True))
        ss.append(qk[h] * (jnp.exp(d - m_t) * dk ** -0.5))
        wis.append(jnp.exp(inter - m_t))
        m_ts.append(m_t)

    kws, w_cs = [], []
    for h in heads:
        b_last = b_cols[h][l - 1:l, :]
        g_col = b_last - b_cols[h] + ig_cols[h]
        m_new = jnp.maximum(b_last + ms[h], jnp.max(g_col, axis=0, keepdims=True))
        w_cs.append(jnp.exp(b_last + ms[h] - m_new))
        kws.append(jnp.exp(g_col - m_new) * (ks[h].astype(F32) * dk ** -0.5))
        m_ref[h:h + 1, :] = jnp.broadcast_to(m_new, (1, m_ref.shape[1]))

    sv = [_dot(s.astype(BF16), v) for s, v in zip(ss, vs)]
    vk = [_dot_tn(v, kw.astype(BF16)) for v, kw in zip(vs, kws)]

    for h in heads:
        c_ref[h] = w_cs[h] * cs[h] + vk[h]
        n_ref[h:h + 1, :] = w_cs[h] * ns[h] + jnp.sum(kws[h], axis=0, keepdims=True)
        num = wis[h] * qc[h] + sv[h]
        qn = jnp.sum(qs[h].astype(F32) * ns[h].astype(BF16).astype(F32), axis=1, keepdims=True)
        den = wis[h] * qn + jnp.sum(ss[h], axis=1, keepdims=True)
        hid = num / jnp.maximum(jnp.abs(den), jnp.exp(-m_ts[h]))
        hid = hid * lax.rsqrt(jnp.mean(hid * hid, axis=1, keepdims=True) + HEAD_EPS)
        gate = jax.nn.sigmoid(o_ref[:, h * dv:(h + 1) * dv])
        y_ref[:, h * dv:(h + 1) * dv] = (hid * gh_ref[:, h * dv:(h + 1) * dv] * gate).astype(BF16)


def _mlstm(q, k, v, o, gates, b_gate, g_head, n_heads, l, n_valid, init=None):
    nb, t, qk_w = q.shape
    v_w = v.shape[2]
    dk = qk_w // n_heads
    dv = v_w // n_heads
    tok = lambda width: pl.BlockSpec((None, l, width), lambda b, c: (b, c, 0))
    const = lambda shape: pl.BlockSpec(shape, lambda b, c: (0,) * len(shape))
    c_spec = pl.BlockSpec((None, n_heads, dv, dk), lambda b, c: (b, 0, 0, 0))
    n_spec = pl.BlockSpec((None, n_heads, dk), lambda b, c: (b, 0, 0))
    m_spec = pl.BlockSpec((None, n_heads, LANES), lambda b, c: (b, 0, 0))
    in_specs = [tok(qk_w), tok(qk_w), tok(v_w), tok(v_w), tok(LANES), const((1, LANES)), const((1, v_w))]
    args = [q, k, v, o, gates, b_gate, g_head]
    if init is not None:
        in_specs += [c_spec, n_spec, m_spec]
        args += list(init)
    return pl.pallas_call(
        functools.partial(_mlstm_kernel, n_heads=n_heads, n_valid=n_valid, has_init=init is not None),
        out_shape=(
            jax.ShapeDtypeStruct((nb, t, v_w), BF16),
            jax.ShapeDtypeStruct((nb, n_heads, dv, dk), F32),
            jax.ShapeDtypeStruct((nb, n_heads, dk), F32),
            jax.ShapeDtypeStruct((nb, n_heads, LANES), F32),
        ),
        grid=(nb, t // l),
        in_specs=in_specs,
        out_specs=[tok(v_w), c_spec, n_spec, m_spec],
        compiler_params=_params("arbitrary", "arbitrary"),
        name="mlstm",
    )(*args)


def _pick(n, candidates):
    for c in candidates:
        if n % c == 0:
            return c
    return n


def kernel(x_prompt, x_sample, cache_k, cache_v, page_table, state_C, state_n, state_m, state_conv, w_in_a, w_out_a, sb_bias, w_in_b, b_gate_b, g_head_b, w_out_b, w_up, conv_w, conv_b, w_down, ln1_g, ln1_b, ln2_g, ln2_b):
    depth = w_up.shape[0]
    alpha = (2 * depth) ** 0.25
    bp, seq, d_model = x_prompt.shape
    bs, t_new, _ = x_sample.shape
    n_heads_a = sb_bias.shape[1]
    w_a = w_in_a.shape[2] // 3
    dh_a = w_a // n_heads_a
    n_heads_b = state_C.shape[2]
    dv_b, dk_b = state_C.shape[3], state_C.shape[4]
    qk_w, v_w = n_heads_b * dk_b, n_heads_b * dv_b
    d_ff = w_up.shape[2] // 2
    page = cache_k.shape[2]
    assert bs == SUBLANES and t_new <= SUBLANES and w_a == d_model

    mp = bp * seq
    ms = bs * t_new
    tm_p = _pick(seq, TM_PROMPT)
    tn_f = _pick(d_ff, TN_FF)

    xp = x_prompt.reshape(mp, d_model)
    xs = x_sample.transpose(1, 0, 2).reshape(ms, d_model)

    def to_batch_major(a, pad_to=None):
        a = a.reshape(t_new, bs, -1).transpose(1, 0, 2)
        if pad_to is not None:
            a = jnp.pad(a, ((0, 0), (0, pad_to - t_new), (0, 0)))
        return a

    def to_time_major(a):
        return a[:, :t_new].transpose(1, 0, 2).reshape(ms, -1)

    outs_p = dict(k=[], v=[], c=[], n=[], m=[], conv=[])
    outs_s = dict(k=[], v=[], c=[], n=[], m=[], conv=[])
    xp_f, xs_f = xp, xs
    xp_b = xs_b = None

    w_in_a_b, w_out_a_b = w_in_a.astype(BF16), w_out_a.astype(BF16)
    w_in_b_b, w_out_b_b = w_in_b.astype(BF16), w_out_b.astype(BF16)
    w_down_b = w_down.astype(BF16)
    q_scale = dh_a ** -0.5 * LOG2_E

    for i in range(depth):
        j = i // N_MIXERS
        if i % N_MIXERS == 0:
            w_out = w_out_a_b
            qb, kb, vb, kf, vf = _qkv_proj(xp_f, w_in_a_b, j, q_scale, tm_p, TN_QKV)
            outs_p["k"].append(kf.reshape(bp, seq, n_heads_a, dh_a))
            outs_p["v"].append(vf.reshape(bp, seq, n_heads_a, dh_a))
            shp = (bp, seq, w_a)
            t_attn = _pick(seq, T_ATTN)
            n_sub = _pick(seq // t_attn, ATTN_SUBTILES)
            bias2 = sb_bias[j] * LOG2_E
            mix_p = _sb_prompt(qb.reshape(shp), kb.reshape(shp), vb.reshape(shp), bias2, n_heads_a,
                               t_attn, n_sub).reshape(mp, w_a)
            qb, kb, vb, kf, vf = _qkv_proj(xs_f, w_in_a_b, j, q_scale, ms, TN_QKV)
            outs_s["k"].append(to_batch_major(kf).reshape(bs, t_new, n_heads_a, dh_a))
            outs_s["v"].append(to_batch_major(vf).reshape(bs, t_new, n_heads_a, dh_a))
            q8 = to_batch_major(qb, SUBLANES).reshape(bs, SUBLANES, n_heads_a, dh_a)
            q_hq = q8.transpose(0, 2, 1, 3).reshape(bs, n_heads_a * SUBLANES, dh_a)
            bias_cols = jnp.repeat(bias2, SUBLANES)[:, None]
            n_layers_a, pool = cache_k.shape[0], cache_k.shape[1]
            as_stored = lambda c: c.reshape(n_layers_a * pool, page * n_heads_a, dh_a)
            o_s = _sb_sample(page_table, q_hq, bias_cols, to_batch_major(kb, page), to_batch_major(vb, page),
                             as_stored(cache_k), as_stored(cache_v), j * pool, n_heads_a, t_new,
                             _pick(page_table.shape[1], PAGES_PER_STEP))
            o_s = o_s.reshape(bs, n_heads_a, SUBLANES, dh_a).transpose(0, 2, 1, 3).reshape(bs, SUBLANES, w_a)
            mix_s = to_time_major(o_s).astype(BF16)
        else:
            w_out = w_out_b_b
            n_main = 2 * qk_w + 2 * v_w
            w_gate = jnp.pad(w_in_b_b[j, :, n_main:], ((0, 0), (0, LANES - 2 * n_heads_b)))
            b_gate = jnp.pad(b_gate_b[j], (0, LANES - 2 * n_heads_b))[None, :]
            g_head = g_head_b[j][None, :]
            q, k, v, o, g = _mlstm_proj(xp_b, w_in_b_b, j, w_gate, qk_w, v_w, tm_p, 4)
            r3 = lambda a: a.reshape(bp, seq, -1)
            l = MLSTM_CHUNK if seq % MLSTM_CHUNK == 0 else seq
            mix_p, c, n, m = _mlstm(r3(q), r3(k), r3(v), r3(o), r3(g), b_gate, g_head, n_heads_b, l, l)
            mix_p = mix_p.reshape(mp, v_w)
            outs_p["c"].append(c)
            outs_p["n"].append(n)
            outs_p["m"].append(m[:, :, 0])
            q, k, v, o, g = _mlstm_proj(xs_b, w_in_b_b, j, w_gate, qk_w, v_w, ms, 4)
            ls = MLSTM_CHUNK if t_new % MLSTM_CHUNK == 0 else t_new
            assert ls == t_new and t_new <= SAMPLE_CHUNK
            padl = lambda a: to_batch_major(a, SAMPLE_CHUNK)
            m0 = jnp.broadcast_to(state_m[j][:, :, None], (bs, n_heads_b, LANES))
            y, c, n, m = _mlstm(padl(q), padl(k), padl(v), padl(o), padl(g), b_gate, g_head, n_heads_b,
                                SAMPLE_CHUNK, t_new, init=(state_C[j], state_n[j], m0))
            mix_s = to_time_major(y)
            outs_s["c"].append(c)
            outs_s["n"].append(n)
            outs_s["m"].append(m[:, :, 0])

        g1, b1 = ln1_g[i][None, :], ln1_b[i][None, :]
        g2, b2 = ln2_g[i][None, :], ln2_b[i][None, :]
        cw, cb = conv_w[i], conv_b[i][None, :]

        xp_f, xp_b = _proj_ln(mix_p, w_out, j, xp_f, g1, b1, alpha, _pick(seq, TM_LN))
        tm_f = _pick(seq, TM_FFN)
        gp, conv_p = _ffn_up_prompt(xp_b, w_up, i, cw, cb, bp, tm_f, tn_f, _pick(tm_f, FFN_ROWS))
        outs_p["conv"].append(conv_p)
        xp_f, xp_b = _proj_ln(gp, w_down_b, i, xp_f, g2, b2, alpha, _pick(seq, TM_DOWN))

        xs_f, xs_b = _proj_ln(mix_s, w_out, j, xs_f, g1, b1, alpha, ms)
        gs, conv_s = _ffn_up_sample(xs_b, w_up, i, cw, cb, state_conv[i].transpose(1, 0, 2), t_new, tn_f)
        outs_s["conv"].append(conv_s.transpose(1, 0, 2))
        xs_f, xs_b = _proj_ln(gs, w_down_b, i, xs_f, g2, b2, alpha, ms)

    y_prompt = xp_f.reshape(bp, seq, d_model)
    y_sample = xs_f.reshape(t_new, bs, d_model).transpose(1, 0, 2)
    st = jnp.stack
    return (y_prompt, y_sample, st(outs_p["k"]), st(outs_p["v"]), st(outs_s["k"]), st(outs_s["v"]),
            st(outs_p["c"]), st(outs_p["n"]), st(outs_p["m"]), st(outs_s["c"]), st(outs_s["n"]), st(outs_s["m"]),
            st(outs_p["conv"]), st(outs_s["conv"]))
```

```python
import functools

import jax
import jax.numpy as jnp
from jax import lax
from jax.experimental import pallas as pl
from jax.experimental.pallas import tpu as pltpu

F32 = jnp.float32
BF16 = jnp.bfloat16

LN_EPS = 1e-5
HEAD_EPS = 1e-6
CONV_W = 3
MLSTM_CHUNK = 128
SAMPLE_CHUNK = 16
N_MIXERS = 2
LOG2_E = 1.4426950408889634

SUBLANES = 8
LANES = 128
MXU_DIM = 256
VMEM_LIMIT_BYTES = 56 * 1024 * 1024

TM_PROMPT = (1024, 512, 256, 128)
TM_FFN = (2048, 1024, 512, 256, 128)
TM_LN = (512, 256, 128)
TM_DOWN = (256, 128)
TN_FF = (512, 256, 128)
TN_QKV = 512
T_ATTN = (MXU_DIM, LANES)
FFN_ROWS = (128,)
LN_ROWS = (256, 128)
ATTN_SUBTILES = (4, 2, 1)
ATTN_UNROLL = (4, 2, 1)
PAGES_PER_STEP = (8, 4, 2, 1)


def _params(*semantics):
    return pltpu.CompilerParams(dimension_semantics=semantics, vmem_limit_bytes=VMEM_LIMIT_BYTES)


def _dot(a, b):
    return jnp.dot(a, b, preferred_element_type=F32)


def _dot_nt(a, b):
    return lax.dot_general(a, b, (((1,), (1,)), ((), ())), preferred_element_type=F32)


def _dot_tn(a, b):
    return lax.dot_general(a, b, (((0,), (0,)), ((), ())), preferred_element_type=F32)


def _split_dot(x, t, n_terms, left=False):
    acc = None
    rem = x
    for i in range(n_terms):
        part = rem.astype(BF16)
        term = _dot(t, part) if left else _dot(part, t)
        acc = term if acc is None else acc + term
        if i + 1 < n_terms:
            rem = rem - part.astype(F32)
    return acc


def _softplus2(z):
    return jnp.maximum(z, 0.0) + jnp.log2(1.0 + jnp.exp2(-jnp.abs(z)))


def _softplus_log1p(z):
    return jnp.maximum(z, 0.0) + jnp.log1p(jnp.exp(-jnp.abs(z)))


def _layer_norm_rows(z, g, b):
    mu = jnp.mean(z, axis=-1, keepdims=True)
    zc = z - mu
    var = jnp.mean(zc * zc, axis=-1, keepdims=True)
    return zc * lax.rsqrt(var + LN_EPS) * g + b


def _qkv_kernel(x_ref, wq_ref, wk_ref, wv_ref, qb_ref, kb_ref, vb_ref, kf_ref, vf_ref, xb_ref, *, q_scale):
    @pl.when(pl.program_id(1) == 0)
    def _():
        xb_ref[...] = x_ref[...].astype(BF16)

    xb = xb_ref[...]
    qb_ref[...] = (_dot(xb, wq_ref[...]) * q_scale).astype(BF16)
    k = _dot(xb, wk_ref[...])
    kf_ref[...] = k
    kb_ref[...] = k.astype(BF16)
    v = _dot(xb, wv_ref[...])
    vf_ref[...] = v
    vb_ref[...] = v.astype(BF16)


def _qkv_proj(x, w, layer, q_scale, tm, tn):
    m, d = x.shape
    wa = w.shape[2] // 3
    nj = wa // tn
    row = lambda i, j: (i, j)
    out_bf = jax.ShapeDtypeStruct((m, wa), BF16)
    out_f = jax.ShapeDtypeStruct((m, wa), F32)
    return pl.pallas_call(
        functools.partial(_qkv_kernel, q_scale=q_scale),
        out_shape=(out_bf, out_bf, out_bf, out_f, out_f),
        grid=(m // tm, nj),
        in_specs=[
            pl.BlockSpec((tm, d), lambda i, j: (i, 0)),
            pl.BlockSpec((None, d, tn), lambda i, j: (layer, 0, j)),
            pl.BlockSpec((None, d, tn), lambda i, j: (layer, 0, nj + j)),
            pl.BlockSpec((None, d, tn), lambda i, j: (layer, 0, 2 * nj + j)),
        ],
        out_specs=[pl.BlockSpec((tm, tn), row)] * 5,
        scratch_shapes=[pltpu.VMEM((tm, d), BF16)],
        compiler_params=_params("arbitrary", "arbitrary"),
        name="qkv_proj",
    )(x, w, w, w)


def _mlstm_proj_kernel(x_ref, wq_ref, wk_ref, wv_ref, wo_ref, wg_ref, q_ref, k_ref, v_ref, o_ref, g_ref):
    xb = x_ref[...]
    q_ref[...] = _dot(xb, wq_ref[...]).astype(BF16)
    k_ref[...] = _dot(xb, wk_ref[...]).astype(BF16)
    v_ref[...] = _dot(xb, wv_ref[...]).astype(BF16)
    o_ref[...] = _dot(xb, wo_ref[...])

    @pl.when(pl.program_id(1) == 0)
    def _():
        g_ref[...] = _dot(xb, wg_ref[...])


def _mlstm_proj(xb, w, layer, w_gate, qk_w, v_w, tm, nj):
    m, d = xb.shape
    tq = qk_w // nj
    tv = v_w // nj
    return pl.pallas_call(
        _mlstm_proj_kernel,
        out_shape=(
            jax.ShapeDtypeStruct((m, qk_w), BF16),
            jax.ShapeDtypeStruct((m, qk_w), BF16),
            jax.ShapeDtypeStruct((m, v_w), BF16),
            jax.ShapeDtypeStruct((m, v_w), F32),
            jax.ShapeDtypeStruct((m, LANES), F32),
        ),
        grid=(m // tm, nj),
        in_specs=[
            pl.BlockSpec((tm, d), lambda i, j: (i, 0)),
            pl.BlockSpec((None, d, tq), lambda i, j: (layer, 0, j)),
            pl.BlockSpec((None, d, tq), lambda i, j: (layer, 0, nj + j)),
            pl.BlockSpec((None, d, tv), lambda i, j: (layer, 0, (2 * qk_w) // tv + j)),
            pl.BlockSpec((None, d, tv), lambda i, j: (layer, 0, (2 * qk_w + v_w) // tv + j)),
            pl.BlockSpec((d, LANES), lambda i, j: (0, 0)),
        ],
        out_specs=[
            pl.BlockSpec((tm, tq), lambda i, j: (i, j)),
            pl.BlockSpec((tm, tq), lambda i, j: (i, j)),
            pl.BlockSpec((tm, tv), lambda i, j: (i, j)),
            pl.BlockSpec((tm, tv), lambda i, j: (i, j)),
            pl.BlockSpec((tm, LANES), lambda i, j: (i, 0)),
        ],
        compiler_params=_params("arbitrary", "arbitrary"),
        name="mlstm_proj",
    )(xb, w, w, w, w, w_gate)


def _proj_ln_kernel(a_ref, w_ref, r_ref, g_ref, b_ref, of_ref, ob_ref, *, alpha, rows):
    for r in range(0, a_ref.shape[0], rows):
        y = _dot(a_ref[r:r + rows, :], w_ref[...])
        out = _layer_norm_rows(alpha * r_ref[r:r + rows, :] + y, g_ref[...], b_ref[...])
        of_ref[r:r + rows, :] = out
        ob_ref[r:r + rows, :] = out.astype(BF16)


def _proj_ln(a, w, layer, resid, g, b, alpha, tm):
    m, kd = a.shape
    d = w.shape[2]
    return pl.pallas_call(
        functools.partial(_proj_ln_kernel, alpha=alpha, rows=_pick(tm, LN_ROWS)),
        out_shape=(jax.ShapeDtypeStruct((m, d), F32), jax.ShapeDtypeStruct((m, d), BF16)),
        grid=(m // tm,),
        in_specs=[
            pl.BlockSpec((tm, kd), lambda i: (i, 0)),
            pl.BlockSpec((None, kd, d), lambda i: (layer, 0, 0), pipeline_mode=pl.Buffered(1)),
            pl.BlockSpec((tm, d), lambda i: (i, 0)),
            pl.BlockSpec((1, d), lambda i: (0, 0)),
            pl.BlockSpec((1, d), lambda i: (0, 0)),
        ],
        out_specs=[pl.BlockSpec((tm, d), lambda i: (i, 0))] * 2,
        compiler_params=_params("arbitrary"),
        name="proj_ln",
    )(a, w, resid, g, b)


def _silu_gate(cg, cv):
    return (cg * jax.nn.sigmoid(cg) * cv).astype(BF16)


def _ffn_up_prompt_kernel(x_ref, wg_ref, wv_ref, cwg_ref, cwv_ref, cbg_ref, cbv_ref,
                          o_ref, sg_ref, sv_ref, wgb_ref, wvb_ref, carry_ref, *, tiles_per_seq, rows):
    i = pl.program_id(1)
    tm = x_ref.shape[0]
    tn = o_ref.shape[1]

    @pl.when(i == 0)
    def _():
        wgb_ref[...] = wg_ref[...].astype(BF16)
        wvb_ref[...] = wv_ref[...].astype(BF16)

    @pl.when((i % tiles_per_seq) == 0)
    def _():
        carry_ref[...] = jnp.zeros_like(carry_ref)

    rowi = lax.broadcasted_iota(jnp.int32, (SUBLANES, tn), 0)

    def conv(u, prev, cw_ref, cb_ref):
        w0 = cw_ref[0:1, :]
        w1 = cw_ref[1:2, :]
        w2 = cw_ref[2:3, :]
        bias = cb_ref[...]
        c = bias + w0 * pltpu.roll(u, 2, 0) + w1 * pltpu.roll(u, 1, 0) + w2 * u
        u8 = u[:SUBLANES, :]
        p0 = prev[SUBLANES - 2:SUBLANES - 1, :]
        p1 = prev[SUBLANES - 1:SUBLANES, :]
        m1 = jnp.where(rowi == 0, p1, pltpu.roll(u8, 1, 0))
        m2 = jnp.where(rowi == 0, p0, jnp.where(rowi == 1, p1, pltpu.roll(u8, 2, 0)))
        c8 = bias + w0 * m2 + w1 * m1 + w2 * u8
        return jnp.concatenate([c8, c[SUBLANES:, :]], axis=0)

    prev_g = carry_ref[0]
    prev_v = carry_ref[1]
    for r in range(0, tm, rows):
        xb = x_ref[r:r + rows, :]
        ug = _dot(xb, wgb_ref[...])
        uv = _dot(xb, wvb_ref[...])
        o_ref[r:r + rows, :] = _silu_gate(conv(ug, prev_g, cwg_ref, cbg_ref), conv(uv, prev_v, cwv_ref, cbv_ref))
        prev_g = ug[rows - SUBLANES:, :]
        prev_v = uv[rows - SUBLANES:, :]
    carry_ref[0] = prev_g
    carry_ref[1] = prev_v
    sg_ref[...] = prev_g[SUBLANES - (CONV_W - 1):, :]
    sv_ref[...] = prev_v[SUBLANES - (CONV_W - 1):, :]


def _ffn_up_prompt(xb, w_up, layer, conv_w, conv_b, batch, tm, tn, rows):
    m, d = xb.shape
    f = w_up.shape[2] // 2
    nj = f // tn
    seq = m // batch
    tiles_per_seq = seq // tm
    state = jax.ShapeDtypeStruct((m // tm, CONV_W - 1, f), F32)
    g, sg, sv = pl.pallas_call(
        functools.partial(_ffn_up_prompt_kernel, tiles_per_seq=tiles_per_seq, rows=rows),
        out_shape=(jax.ShapeDtypeStruct((m, f), BF16), state, state),
        grid=(nj, m // tm),
        in_specs=[
            pl.BlockSpec((tm, d), lambda j, i: (i, 0)),
            pl.BlockSpec((None, d, tn), lambda j, i: (layer, 0, j)),
            pl.BlockSpec((None, d, tn), lambda j, i: (layer, 0, nj + j)),
            pl.BlockSpec((CONV_W, tn), lambda j, i: (0, j)),
            pl.BlockSpec((CONV_W, tn), lambda j, i: (0, nj + j)),
            pl.BlockSpec((1, tn), lambda j, i: (0, j)),
            pl.BlockSpec((1, tn), lambda j, i: (0, nj + j)),
        ],
        out_specs=[
            pl.BlockSpec((tm, tn), lambda j, i: (i, j)),
            pl.BlockSpec((None, CONV_W - 1, tn), lambda j, i: (i, 0, j)),
            pl.BlockSpec((None, CONV_W - 1, tn), lambda j, i: (i, 0, j)),
        ],
        scratch_shapes=[pltpu.VMEM((d, tn), BF16), pltpu.VMEM((d, tn), BF16),
                        pltpu.VMEM((2, SUBLANES, tn), F32)],
        compiler_params=_params("arbitrary", "arbitrary"),
        name="ffn_up_prompt",
    )(xb, w_up, w_up, conv_w, conv_w, conv_b, conv_b)
    last = slice(tiles_per_seq - 1, None, tiles_per_seq)
    return g, jnp.concatenate([sg[last], sv[last]], axis=-1)


def _ffn_up_sample_kernel(x_ref, wg_ref, wv_ref, cwg_ref, cwv_ref, cbg_ref, cbv_ref, pg_ref, pv_ref,
                          o_ref, sg_ref, sv_ref, *, n_t, nb):
    xb = x_ref[...]

    def conv(u, cw_ref, cb_ref, p_ref, s_ref):
        up = [p_ref[0], p_ref[1]] + [u[t * nb:(t + 1) * nb, :] for t in range(n_t)]
        s_ref[0] = up[n_t]
        s_ref[1] = up[n_t + 1]
        bias = cb_ref[...]
        return [bias + cw_ref[0:1, :] * up[t] + cw_ref[1:2, :] * up[t + 1] + cw_ref[2:3, :] * up[t + 2]
                for t in range(n_t)]

    cg = conv(_dot(xb, wg_ref[...].astype(BF16)), cwg_ref, cbg_ref, pg_ref, sg_ref)
    cv = conv(_dot(xb, wv_ref[...].astype(BF16)), cwv_ref, cbv_ref, pv_ref, sv_ref)
    for t in range(n_t):
        o_ref[t * nb:(t + 1) * nb, :] = _silu_gate(cg[t], cv[t])


def _ffn_up_sample(xb, w_up, layer, conv_w, conv_b, prefix, n_t, tn):
    m, d = xb.shape
    nb = m // n_t
    f = w_up.shape[2] // 2
    nj = f // tn
    state = jax.ShapeDtypeStruct((CONV_W - 1, nb, f), F32)
    g, sg, sv = pl.pallas_call(
        functools.partial(_ffn_up_sample_kernel, n_t=n_t, nb=nb),
        out_shape=(jax.ShapeDtypeStruct((m, f), BF16), state, state),
        grid=(nj,),
        in_specs=[
            pl.BlockSpec((m, d), lambda j: (0, 0)),
            pl.BlockSpec((None, d, tn), lambda j: (layer, 0, j)),
            pl.BlockSpec((None, d, tn), lambda j: (layer, 0, nj + j)),
            pl.BlockSpec((CONV_W, tn), lambda j: (0, j)),
            pl.BlockSpec((CONV_W, tn), lambda j: (0, nj + j)),
            pl.BlockSpec((1, tn), lambda j: (0, j)),
            pl.BlockSpec((1, tn), lambda j: (0, nj + j)),
            pl.BlockSpec((CONV_W - 1, nb, tn), lambda j: (0, 0, j)),
            pl.BlockSpec((CONV_W - 1, nb, tn), lambda j: (0, 0, nj + j)),
        ],
        out_specs=[
            pl.BlockSpec((m, tn), lambda j: (0, j)),
            pl.BlockSpec((CONV_W - 1, nb, tn), lambda j: (0, 0, j)),
            pl.BlockSpec((CONV_W - 1, nb, tn), lambda j: (0, 0, j)),
        ],
        compiler_params=_params("arbitrary"),
        name="ffn_up_sample",
    )(xb, w_up, w_up, conv_w, conv_w, conv_b, conv_b, prefix, prefix)
    return g, jnp.concatenate([sg, sv], axis=-1)


def _sb_prompt_kernel(bias_ref, q_ref, k_ref, v_ref, o_ref, acc_ref, run_ref, *, t, n_sub, unroll):
    h = pl.program_id(1)
    qi = pl.program_id(2)
    bias = bias_ref[h]
    row = lax.broadcasted_iota(jnp.int32, (t, t), 0)
    col = lax.broadcasted_iota(jnp.int32, (t, t), 1)
    suffix_incl = (row >= col).astype(BF16)
    valid = col < row

    def process(key_blocks, pairs):
        starts = [pl.multiple_of(j * t, t) for j in key_blocks]
        ks = [k_ref[pl.ds(st, t), :] for st in starts]
        vs = [v_ref[pl.ds(st, t), :] for st in starts]
        zs = [_dot_nt(q_ref[s * t:(s + 1) * t, :], ks[kb]) + bias for kb, s, _ in pairs]
        sps = [_softplus2(z) for z in zs]
        sps = [jnp.where(valid, sp, 0.0) if diagonal else sp for sp, (_, _, diagonal) in zip(sps, pairs)]
        incls = [_dot(sp.astype(BF16), suffix_incl) for sp in sps]
        runs = {s: run_ref[s] for s in sorted({s for _, s, _ in pairs})}
        probs = []
        for (_, s, diagonal), z, incl in zip(pairs, zs, incls):
            a = jnp.exp2(z - incl - runs[s])
            probs.append((jnp.where(valid, a, 0.0) if diagonal else a).astype(BF16))
            runs[s] = runs[s] + incl[:, 0:1]
        outs = [_dot(a, vs[kb]) for a, (kb, _, _) in zip(probs, pairs)]
        for s, run in runs.items():
            run_ref[s] = run
            acc_ref[s] += functools.reduce(lambda x, y: x + y, [o for o, (_, s2, _) in zip(outs, pairs) if s2 == s])

    top = [n_sub * qi + r for r in range(n_sub - 1, -1, -1)]
    acc_ref[...] = jnp.zeros_like(acc_ref)
    run_ref[...] = jnp.zeros_like(run_ref)
    process(top, [(slot, s, s == n_sub - 1 - slot) for slot in range(n_sub) for s in range(n_sub - 1 - slot, n_sub)])

    def body(it, carry):
        first = n_sub * qi - 1 - it * unroll
        process([first - u for u in range(unroll)], [(u, s, False) for u in range(unroll) for s in range(n_sub)])
        return carry

    lax.fori_loop(0, (n_sub * qi) // unroll, body, 0)
    for s in range(n_sub):
        o_ref[s * t:(s + 1) * t, :] = acc_ref[s].astype(BF16)


def _sb_prompt(q, k, v, bias, n_heads, t, n_sub):
    b, s, w = q.shape
    dh = w // n_heads
    tq = t * n_sub
    unroll = _pick(n_sub, ATTN_UNROLL)
    return pl.pallas_call(
        functools.partial(_sb_prompt_kernel, t=t, n_sub=n_sub, unroll=unroll),
        out_shape=jax.ShapeDtypeStruct((b, s, w), BF16),
        grid=(b, n_heads, s // tq),
        in_specs=[
            pl.BlockSpec(memory_space=pltpu.SMEM),
            pl.BlockSpec((None, tq, dh), lambda bi, h, qi: (bi, qi, h)),
            pl.BlockSpec((None, s, dh), lambda bi, h, qi: (bi, 0, h)),
            pl.BlockSpec((None, s, dh), lambda bi, h, qi: (bi, 0, h)),
        ],
        out_specs=pl.BlockSpec((None, tq, dh), lambda bi, h, qi: (bi, qi, h)),
        scratch_shapes=[pltpu.VMEM((n_sub, t, dh), F32), pltpu.VMEM((n_sub, t, 1), F32)],
        compiler_params=_params("arbitrary", "arbitrary", "arbitrary"),
        name="sb_prompt",
    )(bias, q, k, v)


def _sb_sample_kernel(pt_ref, q_ref, bias_ref, kn_ref, vn_ref, spread_ref, same_head_ref, *refs,
                      n_heads, n_new, pages_per_step):
    kp_refs = refs[:pages_per_step]
    vp_refs = refs[pages_per_step:2 * pages_per_step]
    o_ref, acc_ref, run_ref, probs_ref = refs[2 * pages_per_step:]
    step = pl.program_id(1)
    page = kp_refs[0].shape[0] // n_heads
    q = q_ref[...]
    bias = bias_ref[...]
    ncol = q.shape[0]
    row = lax.broadcasted_iota(jnp.int32, (page, page), 0)
    col = lax.broadcasted_iota(jnp.int32, (page, page), 1)
    suffix_incl = (row >= col).astype(BF16)

    def head_rows(a, h):
        return a[h * SUBLANES:(h + 1) * SUBLANES]

    def weigh_head_major(a, val):
        cols = []
        for h in range(n_heads):
            parts = []
            if h > 0:
                parts.append(jnp.zeros((h * SUBLANES, page), F32))
            parts.append(head_rows(a, h))
            if h + 1 < n_heads:
                parts.append(jnp.zeros((ncol - (h + 1) * SUBLANES, page), F32))
            cols.append(jnp.concatenate(parts, axis=0))
        return _dot(jnp.concatenate(cols, axis=1).astype(BF16), val)

    def weigh_stored(a, val):
        spread = _dot(a, spread_ref[...]).astype(BF16) * same_head_ref[...]
        return _dot(spread, val)

    def weights(keys, valid, run):
        zt = _dot_nt(q, keys)
        z = jnp.concatenate([head_rows(zt, h)[:, h * page:(h + 1) * page] for h in range(n_heads)], axis=0) + bias
        sp = _softplus2(z)
        if valid is not None:
            sp = jnp.where(valid, sp, 0.0)
        incl = _split_dot(sp, suffix_incl, 2)
        a = jnp.exp2(z - incl - run)
        if valid is not None:
            a = jnp.where(valid, a, 0.0)
        return a, run + incl[:, 0:1]

    @pl.when(step == 0)
    def _():
        qry = lax.broadcasted_iota(jnp.int32, (ncol, page), 0) % SUBLANES
        key = lax.broadcasted_iota(jnp.int32, (ncol, page), 1)
        by_head = lambda r: jnp.concatenate([r[:, h * q.shape[1]:(h + 1) * q.shape[1]] for h in range(n_heads)], axis=0)
        a, run = weights(by_head(kn_ref[...]), (key < qry) & (qry < n_new), jnp.zeros((ncol, 1), F32))
        acc_ref[...] = weigh_head_major(a, by_head(vn_ref[...]))
        run_ref[...] = run

        @pl.when(pl.program_id(0) == 0)
        def _():
            probs_ref[...] = jnp.zeros_like(probs_ref)

    def keys_head_major(ref):
        heads = [ref[pl.ds(h, page, stride=n_heads), :] for h in range(n_heads)]
        return jnp.concatenate(heads, axis=0).astype(BF16)

    run = run_ref[...]
    total = jnp.zeros(acc_ref.shape, F32)
    for i in range(pages_per_step):
        total = total + weigh_stored(probs_ref[i], vp_refs[i][...].astype(BF16))
        a, run = weights(keys_head_major(kp_refs[i]), None, run)
        probs_ref[i] = a.astype(BF16)
    run_ref[...] = run
    acc_ref[...] += jnp.where(step > 0, total, 0.0)

    @pl.when(step == pl.num_programs(1) - 1)
    def _():
        o_ref[...] = acc_ref[...]


def _sb_sample(page_table, q, bias_cols, k_new, v_new, cache_k, cache_v, first_page, n_heads, n_new,
               pages_per_step):
    nb, n_pages = page_table.shape
    ncol, dh = q.shape[1], q.shape[2]
    w = n_heads * dh
    rows = cache_k.shape[1]
    page = rows // n_heads
    last = n_pages - 1

    n_steps = n_pages // pages_per_step

    def page_spec(i, lag):
        def index(b, p, pt):
            group = jnp.clip(p - lag, 0, n_steps - 1)
            return (pt[b, last - (group * pages_per_step + i)] + first_page, 0, 0)
        return pl.BlockSpec((None, rows, dh), index)

    c_tok = jnp.arange(rows)[None, :] // n_heads
    c_head = jnp.arange(rows)[None, :] % n_heads
    spread = (c_tok == jnp.arange(page)[:, None]).astype(BF16)
    same_head = (c_head == jnp.arange(ncol)[:, None] // SUBLANES).astype(BF16)
    const = lambda shape: pl.BlockSpec(shape, lambda b, p, pt: (0, 0))

    grid_spec = pltpu.PrefetchScalarGridSpec(
        num_scalar_prefetch=1,
        grid=(nb, n_steps + 1),
        in_specs=[
            pl.BlockSpec((None, ncol, dh), lambda b, p, pt: (b, 0, 0)),
            const((ncol, 1)),
            pl.BlockSpec((None, page, w), lambda b, p, pt: (b, 0, 0)),
            pl.BlockSpec((None, page, w), lambda b, p, pt: (b, 0, 0)),
            const((page, rows)),
            const((ncol, rows)),
        ] + [page_spec(i, 0) for i in range(pages_per_step)] + [page_spec(i, 1) for i in range(pages_per_step)],
        out_specs=pl.BlockSpec((None, ncol, dh), lambda b, p, pt: (b, 0, 0)),
        scratch_shapes=[pltpu.VMEM((ncol, dh), F32), pltpu.VMEM((ncol, 1), F32),
                        pltpu.VMEM((pages_per_step, ncol, page), BF16)],
    )
    return pl.pallas_call(
        functools.partial(_sb_sample_kernel, n_heads=n_heads, n_new=n_new,
                          pages_per_step=pages_per_step),
        out_shape=jax.ShapeDtypeStruct((nb, ncol, dh), F32),
        grid_spec=grid_spec,
        compiler_params=_params("arbitrary", "arbitrary"),
        name="sb_sample",
    )(page_table, q, bias_cols, k_new, v_new, spread, same_head,
      *([cache_k] * pages_per_step), *([cache_v] * pages_per_step))


def _mlstm_kernel(*refs, n_heads, n_valid, has_init):
    if has_init:
        (q_ref, k_ref, v_ref, o_ref, g_ref, bg_ref, gh_ref, c0_ref, n0_ref, m0_ref,
         y_ref, c_ref, n_ref, m_ref) = refs
    else:
        q_ref, k_ref, v_ref, o_ref, g_ref, bg_ref, gh_ref, y_ref, c_ref, n_ref, m_ref = refs
    chunk = pl.program_id(1)
    l = q_ref.shape[0]
    dk = q_ref.shape[1] // n_heads
    dv = v_ref.shape[1] // n_heads

    @pl.when(chunk == 0)
    def _():
        if has_init:
            c_ref[...] = c0_ref[...]
            n_ref[...] = n0_ref[...]
            m_ref[...] = m0_ref[...]
        else:
            c_ref[...] = jnp.zeros_like(c_ref)
            n_ref[...] = jnp.zeros_like(n_ref)
            m_ref[...] = jnp.zeros_like(m_ref)

    gates = g_ref[...] + bg_ref[...]
    lane = lax.broadcasted_iota(jnp.int32, gates.shape, 1)
    tok = lax.broadcasted_iota(jnp.int32, gates.shape, 0)
    is_forget = (lane >= n_heads) & (lane < 2 * n_heads)
    log_f = jnp.where(is_forget, -_softplus_log1p(-gates), 0.0)
    if n_valid < l:
        log_f = jnp.where(tok < n_valid, log_f, 0.0)
        gates = jnp.where(tok < n_valid, gates, -jnp.inf)
    ti = lax.broadcasted_iota(jnp.int32, (l, l), 0)
    si = lax.broadcasted_iota(jnp.int32, (l, l), 1)
    causal = si <= ti
    diag = si == ti
    tri = causal.astype(BF16)
    cum_col = _split_dot(log_f, tri, 3, left=True)

    def to_row(column):
        return jnp.sum(jnp.where(diag, column, 0.0), axis=0, keepdims=True)

    heads = range(n_heads)
    qs = [q_ref[:, h * dk:(h + 1) * dk] for h in heads]
    ks = [k_ref[:, h * dk:(h + 1) * dk] for h in heads]
    vs = [v_ref[:, h * dv:(h + 1) * dv] for h in heads]
    cs = [c_ref[h] for h in heads]
    ns = [n_ref[h:h + 1, :] for h in heads]
    ms = [m_ref[h:h + 1, 0:1] for h in heads]
    b_cols = [cum_col[:, n_heads + h:n_heads + h + 1] for h in heads]
    ig_cols = [gates[:, h:h + 1] for h in heads]

    qk = [_dot_nt(q, k) for q, k in zip(qs, ks)]
    qc = [_dot_nt(q, c.astype(BF16)) for q, c in zip(qs, cs)]

    ss, wis, m_ts = [], [], []
    for h in heads:
        d = jnp.where(causal, b_cols[h] - to_row(b_cols[h]) + to_row(ig_cols[h]), -jnp.inf)
        inter = b_cols[h] + ms[h]
        m_t = jnp.maximum(inter, jnp.max(d, axis=1, keepdims=True))
        ss.append(qk[h] * (jnp.exp(d - m_t) * dk ** -0.5))
        wis.append(jnp.exp(inter - m_t))
        m_ts.append(m_t)

    kws, w_cs = [], []
    for h in heads:
        b_last = b_cols[h][l - 1:l, :]
        g_col = b_last - b_cols[h] + ig_cols[h]
        m_new = jnp.maximum(b_last + ms[h], jnp.max(g_col, axis=0, keepdims=True))
        w_cs.append(jnp.exp(b_last + ms[h] - m_new))
        kws.append(jnp.exp(g_col - m_new) * (ks[h].astype(F32) * dk ** -0.5))
        m_ref[h:h + 1, :] = jnp.broadcast_to(m_new, (1, m_ref.shape[1]))

    sv = [_dot(s.astype(BF16), v) for s, v in zip(ss, vs)]
    vk = [_dot_tn(v, kw.astype(BF16)) for v, kw in zip(vs, kws)]

    for h in heads:
        c_ref[h] = w_cs[h] * cs[h] + vk[h]
        n_ref[h:h + 1, :] = w_cs[h] * ns[h] + jnp.sum(kws[h], axis=0, keepdims=True)
        num = wis[h] * qc[h] + sv[h]
        qn = jnp.sum(qs[h].astype(F32) * ns[h].astype(BF16).astype(F32), axis=1, keepdims=True)
        den = wis[h] * qn + jnp.sum(ss[h], axis=1, keepdims=True)
        hid = num / jnp.maximum(jnp.abs(den), jnp.exp(-m_ts[h]))
        hid = hid * lax.rsqrt(jnp.mean(hid * hid, axis=1, keepdims=True) + HEAD_EPS)
        gate = jax.nn.sigmoid(o_ref[:, h * dv:(h + 1) * dv])
        y_ref[:, h * dv:(h + 1) * dv] = (hid * gh_ref[:, h * dv:(h + 1) * dv] * gate).astype(BF16)


def _mlstm(q, k, v, o, gates, b_gate, g_head, n_heads, l, n_valid, init=None):
    nb, t, qk_w = q.shape
    v_w = v.shape[2]
    dk = qk_w // n_heads
    dv = v_w // n_heads
    tok = lambda width: pl.BlockSpec((None, l, width), lambda b, c: (b, c, 0))
    const = lambda shape: pl.BlockSpec(shape, lambda b, c: (0,) * len(shape))
    c_spec = pl.BlockSpec((None, n_heads, dv, dk), lambda b, c: (b, 0, 0, 0))
    n_spec = pl.BlockSpec((None, n_heads, dk), lambda b, c: (b, 0, 0))
    m_spec = pl.BlockSpec((None, n_heads, LANES), lambda b, c: (b, 0, 0))
    in_specs = [tok(qk_w), tok(qk_w), tok(v_w), tok(v_w), tok(LANES), const((1, LANES)), const((1, v_w))]
    args = [q, k, v, o, gates, b_gate, g_head]
    if init is not None:
        in_specs += [c_spec, n_spec, m_spec]
        args += list(init)
    return pl.pallas_call(
        functools.partial(_mlstm_kernel, n_heads=n_heads, n_valid=n_valid, has_init=init is not None),
        out_shape=(
            jax.ShapeDtypeStruct((nb, t, v_w), BF16),
            jax.ShapeDtypeStruct((nb, n_heads, dv, dk), F32),
            jax.ShapeDtypeStruct((nb, n_heads, dk), F32),
            jax.ShapeDtypeStruct((nb, n_heads, LANES), F32),
        ),
        grid=(nb, t // l),
        in_specs=in_specs,
        out_specs=[tok(v_w), c_spec, n_spec, m_spec],
        compiler_params=_params("arbitrary", "arbitrary"),
        name="mlstm",
    )(*args)


def _pick(n, candidates):
    for c in candidates:
        if n % c == 0:
            return c
    return n


def kernel(x_prompt, x_sample, cache_k, cache_v, page_table, state_C, state_n, state_m, state_conv, w_in_a, w_out_a, sb_bias, w_in_b, b_gate_b, g_head_b, w_out_b, w_up, conv_w, conv_b, w_down, ln1_g, ln1_b, ln2_g, ln2_b):
    depth = w_up.shape[0]
    alpha = (2 * depth) ** 0.25
    bp, seq, d_model = x_prompt.shape
    bs, t_new, _ = x_sample.shape
    n_heads_a = sb_bias.shape[1]
    w_a = w_in_a.shape[2] // 3
    dh_a = w_a // n_heads_a
    n_heads_b = state_C.shape[2]
    dv_b, dk_b = state_C.shape[3], state_C.shape[4]
    qk_w, v_w = n_heads_b * dk_b, n_heads_b * dv_b
    d_ff = w_up.shape[2] // 2
    page = cache_k.shape[2]
    assert bs == SUBLANES and t_new <= SUBLANES and w_a == d_model

    mp = bp * seq
    ms = bs * t_new
    tm_p = _pick(seq, TM_PROMPT)
    tn_f = _pick(d_ff, TN_FF)

    xp = x_prompt.reshape(mp, d_model)
    xs = x_sample.transpose(1, 0, 2).reshape(ms, d_model)

    def to_batch_major(a, pad_to=None):
        a = a.reshape(t_new, bs, -1).transpose(1, 0, 2)
        if pad_to is not None:
            a = jnp.pad(a, ((0, 0), (0, pad_to - t_new), (0, 0)))
        return a

    def to_time_major(a):
        return a[:, :t_new].transpose(1, 0, 2).reshape(ms, -1)

    outs_p = dict(k=[], v=[], c=[], n=[], m=[], conv=[])
    outs_s = dict(k=[], v=[], c=[], n=[], m=[], conv=[])
    xp_f, xs_f = xp, xs
    xp_b = xs_b = None

    w_in_a_b, w_out_a_b = w_in_a.astype(BF16), w_out_a.astype(BF16)
    w_in_b_b, w_out_b_b = w_in_b.astype(BF16), w_out_b.astype(BF16)
    w_down_b = w_down.astype(BF16)
    q_scale = dh_a ** -0.5 * LOG2_E

    for i in range(depth):
        j = i // N_MIXERS
        if i % N_MIXERS == 0:
            w_out = w_out_a_b
            qb, kb, vb, kf, vf = _qkv_proj(xp_f, w_in_a_b, j, q_scale, tm_p, TN_QKV)
            outs_p["k"].append(kf.reshape(bp, seq, n_heads_a, dh_a))
            outs_p["v"].append(vf.reshape(bp, seq, n_heads_a, dh_a))
            shp = (bp, seq, w_a)
            t_attn = _pick(seq, T_ATTN)
            n_sub = _pick(seq // t_attn, ATTN_SUBTILES)
            bias2 = sb_bias[j] * LOG2_E
            mix_p = _sb_prompt(qb.reshape(shp), kb.reshape(shp), vb.reshape(shp), bias2, n_heads_a,
                               t_attn, n_sub).reshape(mp, w_a)
            qb, kb, vb, kf, vf = _qkv_proj(xs_f, w_in_a_b, j, q_scale, ms, TN_QKV)
            outs_s["k"].append(to_batch_major(kf).reshape(bs, t_new, n_heads_a, dh_a))
            outs_s["v"].append(to_batch_major(vf).reshape(bs, t_new, n_heads_a, dh_a))
            q8 = to_batch_major(qb, SUBLANES).reshape(bs, SUBLANES, n_heads_a, dh_a)
            q_hq = q8.transpose(0, 2, 1, 3).reshape(bs, n_heads_a * SUBLANES, dh_a)
            bias_cols = jnp.repeat(bias2, SUBLANES)[:, None]
            n_layers_a, pool = cache_k.shape[0], cache_k.shape[1]
            as_stored = lambda c: c.reshape(n_layers_a * pool, page * n_heads_a, dh_a)
            o_s = _sb_sample(page_table, q_hq, bias_cols, to_batch_major(kb, page), to_batch_major(vb, page),
                             as_stored(cache_k), as_stored(cache_v), j * pool, n_heads_a, t_new,
                             _pick(page_table.shape[1], PAGES_PER_STEP))
            o_s = o_s.reshape(bs, n_heads_a, SUBLANES, dh_a).transpose(0, 2, 1, 3).reshape(bs, SUBLANES, w_a)
            mix_s = to_time_major(o_s).astype(BF16)
        else:
            w_out = w_out_b_b
            n_main = 2 * qk_w + 2 * v_w
            w_gate = jnp.pad(w_in_b_b[j, :, n_main:], ((0, 0), (0, LANES - 2 * n_heads_b)))
            b_gate = jnp.pad(b_gate_b[j], (0, LANES - 2 * n_heads_b))[None, :]
            g_head = g_head_b[j][None, :]
            q, k, v, o, g = _mlstm_proj(xp_b, w_in_b_b, j, w_gate, qk_w, v_w, tm_p, 4)
            r3 = lambda a: a.reshape(bp, seq, -1)
            l = MLSTM_CHUNK if seq % MLSTM_CHUNK == 0 else seq
            mix_p, c, n, m = _mlstm(r3(q), r3(k), r3(v), r3(o), r3(g), b_gate, g_head, n_heads_b, l, l)
            mix_p = mix_p.reshape(mp, v_w)
            outs_p["c"].append(c)
            outs_p["n"].append(n)
            outs_p["m"].append(m[:, :, 0])
            q, k, v, o, g = _mlstm_proj(xs_b, w_in_b_b, j, w_gate, qk_w, v_w, ms, 4)
            ls = MLSTM_CHUNK if t_new % MLSTM_CHUNK == 0 else t_new
            assert ls == t_new and t_new <= SAMPLE_CHUNK
            padl = lambda a: to_batch_major(a, SAMPLE_CHUNK)
            m0 = jnp.broadcast_to(state_m[j][:, :, None], (bs, n_heads_b, LANES))
            y, c, n, m = _mlstm(padl(q), padl(k), padl(v), padl(o), padl(g), b_gate, g_head, n_heads_b,
                                SAMPLE_CHUNK, t_new, init=(state_C[j], state_n[j], m0))
            mix_s = to_time_major(y)
            outs_s["c"].append(c)
            outs_s["n"].append(n)
            outs_s["m"].append(m[:, :, 0])

        g1, b1 = ln1_g[i][None, :], ln1_b[i][None, :]
        g2, b2 = ln2_g[i][None, :], ln2_b[i][None, :]
        cw, cb = conv_w[i], conv_b[i][None, :]

        xp_f, xp_b = _proj_ln(mix_p, w_out, j, xp_f, g1, b1, alpha, _pick(seq, TM_LN))
        tm_f = _pick(seq, TM_FFN)
        gp, conv_p = _ffn_up_prompt(xp_b, w_up, i, cw, cb, bp, tm_f, tn_f, _pick(tm_f, FFN_ROWS))
        outs_p["conv"].append(conv_p)
        xp_f, xp_b = _proj_ln(gp, w_down_b, i, xp_f, g2, b2, alpha, _pick(seq, TM_DOWN))

        xs_f, xs_b = _proj_ln(mix_s, w_out, j, xs_f, g1, b1, alpha, ms)
        gs, conv_s = _ffn_up_sample(xs_b, w_up, i, cw, cb, state_conv[i].transpose(1, 0, 2), t_new, tn_f)
        outs_s["conv"].append(conv_s.transpose(1, 0, 2))
        xs_f, xs_b = _proj_ln(gs, w_down_b, i, xs_f, g2, b2, alpha, ms)

    y_prompt = xp_f.reshape(bp, seq, d_model)
    y_sample = xs_f.reshape(t_new, bs, d_model).transpose(1, 0, 2)
    st = jnp.stack
    return (y_prompt, y_sample, st(outs_p["k"]), st(outs_p["v"]), st(outs_s["k"]), st(outs_s["v"]),
            st(outs_p["c"]), st(outs_p["n"]), st(outs_p["m"]), st(outs_s["c"]), st(outs_s["n"]), st(outs_s["m"]),
            st(outs_p["conv"]), st(outs_s["conv"]))
```

```python
import functools

import jax
import jax.numpy as jnp
from jax import lax
from jax.experimental import pallas as pl
from jax.experimental.pallas import tpu as pltpu

F32 = jnp.float32
BF16 = jnp.bfloat16

LN_EPS = 1e-5
HEAD_EPS = 1e-6
CONV_W = 3
MLSTM_CHUNK = 128
SAMPLE_CHUNK = 16
N_MIXERS = 2
LOG2_E = 1.4426950408889634

SUBLANES = 8
LANES = 128
MXU_DIM = 256
VMEM_LIMIT_BYTES = 56 * 1024 * 1024

TM_PROMPT = (1024, 512, 256, 128)
TM_FFN = (2048, 1024, 512, 256, 128)
TM_LN = (512, 256, 128)
TM_DOWN = (256, 128)
TN_FF = (512, 256, 128)
TN_QKV = 512
T_ATTN = (MXU_DIM, LANES)
FFN_ROWS = (512, 256, 128)
LN_ROWS = (256, 128)
ATTN_SUBTILES = (4, 2, 1)
ATTN_UNROLL = (4, 2, 1)
PAGES_PER_STEP = (8, 4, 2, 1)


def _params(*semantics):
    return pltpu.CompilerParams(dimension_semantics=semantics, vmem_limit_bytes=VMEM_LIMIT_BYTES)


def _dot(a, b):
    return jnp.dot(a, b, preferred_element_type=F32)


def _dot_nt(a, b):
    return lax.dot_general(a, b, (((1,), (1,)), ((), ())), preferred_element_type=F32)


def _dot_tn(a, b):
    return lax.dot_general(a, b, (((0,), (0,)), ((), ())), preferred_element_type=F32)


def _split_dot(x, t, n_terms, left=False):
    acc = None
    rem = x
    for i in range(n_terms):
        part = rem.astype(BF16)
        term = _dot(t, part) if left else _dot(part, t)
        acc = term if acc is None else acc + term
        if i + 1 < n_terms:
            rem = rem - part.astype(F32)
    return acc


def _softplus2(z):
    return jnp.maximum(z, 0.0) + jnp.log2(1.0 + jnp.exp2(-jnp.abs(z)))


def _softplus_log1p(z):
    return jnp.maximum(z, 0.0) + jnp.log1p(jnp.exp(-jnp.abs(z)))


def _layer_norm_rows(z, g, b):
    mu = jnp.mean(z, axis=-1, keepdims=True)
    zc = z - mu
    var = jnp.mean(zc * zc, axis=-1, keepdims=True)
    return zc * lax.rsqrt(var + LN_EPS) * g + b


def _qkv_kernel(x_ref, wq_ref, wk_ref, wv_ref, qb_ref, kb_ref, vb_ref, kf_ref, vf_ref, xb_ref, *, q_scale):
    @pl.when(pl.program_id(1) == 0)
    def _():
        xb_ref[...] = x_ref[...].astype(BF16)

    xb = xb_ref[...]
    qb_ref[...] = (_dot(xb, wq_ref[...]) * q_scale).astype(BF16)
    k = _dot(xb, wk_ref[...])
    kf_ref[...] = k
    kb_ref[...] = k.astype(BF16)
    v = _dot(xb, wv_ref[...])
    vf_ref[...] = v
    vb_ref[...] = v.astype(BF16)


def _qkv_proj(x, w, layer, q_scale, tm, tn):
    m, d = x.shape
    wa = w.shape[2] // 3
    nj = wa // tn
    row = lambda i, j: (i, j)
    out_bf = jax.ShapeDtypeStruct((m, wa), BF16)
    out_f = jax.ShapeDtypeStruct((m, wa), F32)
    return pl.pallas_call(
        functools.partial(_qkv_kernel, q_scale=q_scale),
        out_shape=(out_bf, out_bf, out_bf, out_f, out_f),
        grid=(m // tm, nj),
        in_specs=[
            pl.BlockSpec((tm, d), lambda i, j: (i, 0)),
            pl.BlockSpec((None, d, tn), lambda i, j: (layer, 0, j)),
            pl.BlockSpec((None, d, tn), lambda i, j: (layer, 0, nj + j)),
            pl.BlockSpec((None, d, tn), lambda i, j: (layer, 0, 2 * nj + j)),
        ],
        out_specs=[pl.BlockSpec((tm, tn), row)] * 5,
        scratch_shapes=[pltpu.VMEM((tm, d), BF16)],
        compiler_params=_params("arbitrary", "arbitrary"),
        name="qkv_proj",
    )(x, w, w, w)


def _mlstm_proj_kernel(x_ref, wq_ref, wk_ref, wv_ref, wo_ref, wg_ref, q_ref, k_ref, v_ref, o_ref, g_ref):
    xb = x_ref[...]
    q_ref[...] = _dot(xb, wq_ref[...]).astype(BF16)
    k_ref[...] = _dot(xb, wk_ref[...]).astype(BF16)
    v_ref[...] = _dot(xb, wv_ref[...]).astype(BF16)
    o_ref[...] = _dot(xb, wo_ref[...])

    @pl.when(pl.program_id(1) == 0)
    def _():
        g_ref[...] = _dot(xb, wg_ref[...])


def _mlstm_proj(xb, w, layer, w_gate, qk_w, v_w, tm, nj):
    m, d = xb.shape
    tq = qk_w // nj
    tv = v_w // nj
    return pl.pallas_call(
        _mlstm_proj_kernel,
        out_shape=(
            jax.ShapeDtypeStruct((m, qk_w), BF16),
            jax.ShapeDtypeStruct((m, qk_w), BF16),
            jax.ShapeDtypeStruct((m, v_w), BF16),
            jax.ShapeDtypeStruct((m, v_w), F32),
            jax.ShapeDtypeStruct((m, LANES), F32),
        ),
        grid=(m // tm, nj),
        in_specs=[
            pl.BlockSpec((tm, d), lambda i, j: (i, 0)),
            pl.BlockSpec((None, d, tq), lambda i, j: (layer, 0, j)),
            pl.BlockSpec((None, d, tq), lambda i, j: (layer, 0, nj + j)),
            pl.BlockSpec((None, d, tv), lambda i, j: (layer, 0, (2 * qk_w) // tv + j)),
            pl.BlockSpec((None, d, tv), lambda i, j: (layer, 0, (2 * qk_w + v_w) // tv + j)),
            pl.BlockSpec((d, LANES), lambda i, j: (0, 0)),
        ],
        out_specs=[
            pl.BlockSpec((tm, tq), lambda i, j: (i, j)),
            pl.BlockSpec((tm, tq), lambda i, j: (i, j)),
            pl.BlockSpec((tm, tv), lambda i, j: (i, j)),
            pl.BlockSpec((tm, tv), lambda i, j: (i, j)),
            pl.BlockSpec((tm, LANES), lambda i, j: (i, 0)),
        ],
        compiler_params=_params("arbitrary", "arbitrary"),
        name="mlstm_proj",
    )(xb, w, w, w, w, w_gate)


def _proj_ln_kernel(a_ref, w_ref, r_ref, g_ref, b_ref, of_ref, ob_ref, *, alpha, rows):
    for r in range(0, a_ref.shape[0], rows):
        y = _dot(a_ref[r:r + rows, :], w_ref[...])
        out = _layer_norm_rows(alpha * r_ref[r:r + rows, :] + y, g_ref[...], b_ref[...])
        of_ref[r:r + rows, :] = out
        ob_ref[r:r + rows, :] = out.astype(BF16)


def _proj_ln(a, w, layer, resid, g, b, alpha, tm):
    m, kd = a.shape
    d = w.shape[2]
    return pl.pallas_call(
        functools.partial(_proj_ln_kernel, alpha=alpha, rows=_pick(tm, LN_ROWS)),
        out_shape=(jax.ShapeDtypeStruct((m, d), F32), jax.ShapeDtypeStruct((m, d), BF16)),
        grid=(m // tm,),
        in_specs=[
            pl.BlockSpec((tm, kd), lambda i: (i, 0)),
            pl.BlockSpec((None, kd, d), lambda i: (layer, 0, 0), pipeline_mode=pl.Buffered(1)),
            pl.BlockSpec((tm, d), lambda i: (i, 0)),
            pl.BlockSpec((1, d), lambda i: (0, 0)),
            pl.BlockSpec((1, d), lambda i: (0, 0)),
        ],
        out_specs=[pl.BlockSpec((tm, d), lambda i: (i, 0))] * 2,
        compiler_params=_params("arbitrary"),
        name="proj_ln",
    )(a, w, resid, g, b)


def _silu_gate(cg, cv):
    return (cg * jax.nn.sigmoid(cg) * cv).astype(BF16)


def _ffn_up_prompt_kernel(x_ref, wg_ref, wv_ref, cwg_ref, cwv_ref, cbg_ref, cbv_ref,
                          o_ref, sg_ref, sv_ref, wgb_ref, wvb_ref, carry_ref, *, tiles_per_seq, rows):
    i = pl.program_id(1)
    tm = x_ref.shape[0]
    tn = o_ref.shape[1]

    @pl.when(i == 0)
    def _():
        wgb_ref[...] = wg_ref[...].astype(BF16)
        wvb_ref[...] = wv_ref[...].astype(BF16)

    @pl.when((i % tiles_per_seq) == 0)
    def _():
        carry_ref[...] = jnp.zeros_like(carry_ref)

    rowi = lax.broadcasted_iota(jnp.int32, (SUBLANES, tn), 0)

    def conv(u, prev, cw_ref, cb_ref):
        w0 = cw_ref[0:1, :]
        w1 = cw_ref[1:2, :]
        w2 = cw_ref[2:3, :]
        bias = cb_ref[...]
        c = bias + w0 * pltpu.roll(u, 2, 0) + w1 * pltpu.roll(u, 1, 0) + w2 * u
        u8 = u[:SUBLANES, :]
        p0 = prev[SUBLANES - 2:SUBLANES - 1, :]
        p1 = prev[SUBLANES - 1:SUBLANES, :]
        m1 = jnp.where(rowi == 0, p1, pltpu.roll(u8, 1, 0))
        m2 = jnp.where(rowi == 0, p0, jnp.where(rowi == 1, p1, pltpu.roll(u8, 2, 0)))
        c8 = bias + w0 * m2 + w1 * m1 + w2 * u8
        return jnp.concatenate([c8, c[SUBLANES:, :]], axis=0)

    prev_g = carry_ref[0]
    prev_v = carry_ref[1]
    for r in range(0, tm, rows):
        xb = x_ref[r:r + rows, :]
        ug = _dot(xb, wgb_ref[...])
        uv = _dot(xb, wvb_ref[...])
        o_ref[r:r + rows, :] = _silu_gate(conv(ug, prev_g, cwg_ref, cbg_ref), conv(uv, prev_v, cwv_ref, cbv_ref))
        prev_g = ug[rows - SUBLANES:, :]
        prev_v = uv[rows - SUBLANES:, :]
    carry_ref[0] = prev_g
    carry_ref[1] = prev_v
    sg_ref[...] = prev_g[SUBLANES - (CONV_W - 1):, :]
    sv_ref[...] = prev_v[SUBLANES - (CONV_W - 1):, :]


def _ffn_up_prompt(xb, w_up, layer, conv_w, conv_b, batch, tm, tn, rows):
    m, d = xb.shape
    f = w_up.shape[2] // 2
    nj = f // tn
    seq = m // batch
    tiles_per_seq = seq // tm
    state = jax.ShapeDtypeStruct((m // tm, CONV_W - 1, f), F32)
    g, sg, sv = pl.pallas_call(
        functools.partial(_ffn_up_prompt_kernel, tiles_per_seq=tiles_per_seq, rows=rows),
        out_shape=(jax.ShapeDtypeStruct((m, f), BF16), state, state),
        grid=(nj, m // tm),
        in_specs=[
            pl.BlockSpec((tm, d), lambda j, i: (i, 0)),
            pl.BlockSpec((None, d, tn), lambda j, i: (layer, 0, j)),
            pl.BlockSpec((None, d, tn), lambda j, i: (layer, 0, nj + j)),
            pl.BlockSpec((CONV_W, tn), lambda j, i: (0, j)),
            pl.BlockSpec((CONV_W, tn), lambda j, i: (0, nj + j)),
            pl.BlockSpec((1, tn), lambda j, i: (0, j)),
            pl.BlockSpec((1, tn), lambda j, i: (0, nj + j)),
        ],
        out_specs=[
            pl.BlockSpec((tm, tn), lambda j, i: (i, j)),
            pl.BlockSpec((None, CONV_W - 1, tn), lambda j, i: (i, 0, j)),
            pl.BlockSpec((None, CONV_W - 1, tn), lambda j, i: (i, 0, j)),
        ],
        scratch_shapes=[pltpu.VMEM((d, tn), BF16), pltpu.VMEM((d, tn), BF16),
                        pltpu.VMEM((2, SUBLANES, tn), F32)],
        compiler_params=_params("arbitrary", "arbitrary"),
        name="ffn_up_prompt",
    )(xb, w_up, w_up, conv_w, conv_w, conv_b, conv_b)
    last = slice(tiles_per_seq - 1, None, tiles_per_seq)
    return g, jnp.concatenate([sg[last], sv[last]], axis=-1)


def _ffn_up_sample_kernel(x_ref, wg_ref, wv_ref, cwg_ref, cwv_ref, cbg_ref, cbv_ref, pg_ref, pv_ref,
                          o_ref, sg_ref, sv_ref, *, n_t, nb):
    xb = x_ref[...]

    def conv(u, cw_ref, cb_ref, p_ref, s_ref):
        up = [p_ref[0], p_ref[1]] + [u[t * nb:(t + 1) * nb, :] for t in range(n_t)]
        s_ref[0] = up[n_t]
        s_ref[1] = up[n_t + 1]
        bias = cb_ref[...]
        return [bias + cw_ref[0:1, :] * up[t] + cw_ref[1:2, :] * up[t + 1] + cw_ref[2:3, :] * up[t + 2]
                for t in range(n_t)]

    cg = conv(_dot(xb, wg_ref[...].astype(BF16)), cwg_ref, cbg_ref, pg_ref, sg_ref)
    cv = conv(_dot(xb, wv_ref[...].astype(BF16)), cwv_ref, cbv_ref, pv_ref, sv_ref)
    for t in range(n_t):
        o_ref[t * nb:(t + 1) * nb, :] = _silu_gate(cg[t], cv[t])


def _ffn_up_sample(xb, w_up, layer, conv_w, conv_b, prefix, n_t, tn):
    m, d = xb.shape
    nb = m // n_t
    f = w_up.shape[2] // 2
    nj = f // tn
    state = jax.ShapeDtypeStruct((CONV_W - 1, nb, f), F32)
    g, sg, sv = pl.pallas_call(
        functools.partial(_ffn_up_sample_kernel, n_t=n_t, nb=nb),
        out_shape=(jax.ShapeDtypeStruct((m, f), BF16), state, state),
        grid=(nj,),
        in_specs=[
            pl.BlockSpec((m, d), lambda j: (0, 0)),
            pl.BlockSpec((None, d, tn), lambda j: (layer, 0, j)),
            pl.BlockSpec((None, d, tn), lambda j: (layer, 0, nj + j)),
            pl.BlockSpec((CONV_W, tn), lambda j: (0, j)),
            pl.BlockSpec((CONV_W, tn), lambda j: (0, nj + j)),
            pl.BlockSpec((1, tn), lambda j: (0, j)),
            pl.BlockSpec((1, tn), lambda j: (0, nj + j)),
            pl.BlockSpec((CONV_W - 1, nb, tn), lambda j: (0, 0, j)),
            pl.BlockSpec((CONV_W - 1, nb, tn), lambda j: (0, 0, nj + j)),
        ],
        out_specs=[
            pl.BlockSpec((m, tn), lambda j: (0, j)),
            pl.BlockSpec((CONV_W - 1, nb, tn), lambda j: (0, 0, j)),
            pl.BlockSpec((CONV_W - 1, nb, tn), lambda j: (0, 0, j)),
        ],
        compiler_params=_params("arbitrary"),
        name="ffn_up_sample",
    )(xb, w_up, w_up, conv_w, conv_w, conv_b, conv_b, prefix, prefix)
    return g, jnp.concatenate([sg, sv], axis=-1)


def _sb_prompt_kernel(bias_ref, q_ref, k_ref, v_ref, o_ref, acc_ref, run_ref, *, t, n_sub, unroll):
    h = pl.program_id(1)
    qi = pl.program_id(2)
    bias = bias_ref[h]
    row = lax.broadcasted_iota(jnp.int32, (t, t), 0)
    col = lax.broadcasted_iota(jnp.int32, (t, t), 1)
    suffix_incl = (row >= col).astype(BF16)
    valid = col < row

    def process(key_blocks, pairs):
        starts = [pl.multiple_of(j * t, t) for j in key_blocks]
        ks = [k_ref[pl.ds(st, t), :] for st in starts]
        vs = [v_ref[pl.ds(st, t), :] for st in starts]
        zs = [_dot_nt(q_ref[s * t:(s + 1) * t, :], ks[kb]) + bias for kb, s, _ in pairs]
        sps = [_softplus2(z) for z in zs]
        sps = [jnp.where(valid, sp, 0.0) if diagonal else sp for sp, (_, _, diagonal) in zip(sps, pairs)]
        incls = [_dot(sp.astype(BF16), suffix_incl) for sp in sps]
        runs = {s: run_ref[s] for s in sorted({s for _, s, _ in pairs})}
        probs = []
        for (_, s, diagonal), z, incl in zip(pairs, zs, incls):
            a = jnp.exp2(z - incl - runs[s])
            probs.append((jnp.where(valid, a, 0.0) if diagonal else a).astype(BF16))
            runs[s] = runs[s] + incl[:, 0:1]
        outs = [_dot(a, vs[kb]) for a, (kb, _, _) in zip(probs, pairs)]
        for s, run in runs.items():
            run_ref[s] = run
            acc_ref[s] += functools.reduce(lambda x, y: x + y, [o for o, (_, s2, _) in zip(outs, pairs) if s2 == s])

    top = [n_sub * qi + r for r in range(n_sub - 1, -1, -1)]
    acc_ref[...] = jnp.zeros_like(acc_ref)
    run_ref[...] = jnp.zeros_like(run_ref)
    process(top, [(slot, s, s == n_sub - 1 - slot) for slot in range(n_sub) for s in range(n_sub - 1 - slot, n_sub)])

    def body(it, carry):
        first = n_sub * qi - 1 - it * unroll
        process([first - u for u in range(unroll)], [(u, s, False) for u in range(unroll) for s in range(n_sub)])
        return carry

    lax.fori_loop(0, (n_sub * qi) // unroll, body, 0)
    for s in range(n_sub):
        o_ref[s * t:(s + 1) * t, :] = acc_ref[s].astype(BF16)


def _sb_prompt(q, k, v, bias, n_heads, t, n_sub):
    b, s, w = q.shape
    dh = w // n_heads
    tq = t * n_sub
    unroll = _pick(n_sub, ATTN_UNROLL)
    return pl.pallas_call(
        functools.partial(_sb_prompt_kernel, t=t, n_sub=n_sub, unroll=unroll),
        out_shape=jax.ShapeDtypeStruct((b, s, w), BF16),
        grid=(b, n_heads, s // tq),
        in_specs=[
            pl.BlockSpec(memory_space=pltpu.SMEM),
            pl.BlockSpec((None, tq, dh), lambda bi, h, qi: (bi, qi, h)),
            pl.BlockSpec((None, s, dh), lambda bi, h, qi: (bi, 0, h)),
            pl.BlockSpec((None, s, dh), lambda bi, h, qi: (bi, 0, h)),
        ],
        out_specs=pl.BlockSpec((None, tq, dh), lambda bi, h, qi: (bi, qi, h)),
        scratch_shapes=[pltpu.VMEM((n_sub, t, dh), F32), pltpu.VMEM((n_sub, t, 1), F32)],
        compiler_params=_params("arbitrary", "arbitrary", "arbitrary"),
        name="sb_prompt",
    )(bias, q, k, v)


def _sb_sample_kernel(pt_ref, q_ref, bias_ref, kn_ref, vn_ref, spread_ref, same_head_ref, *refs,
                      n_heads, n_new, pages_per_step):
    kp_refs = refs[:pages_per_step]
    vp_refs = refs[pages_per_step:2 * pages_per_step]
    o_ref, acc_ref, run_ref, probs_ref = refs[2 * pages_per_step:]
    step = pl.program_id(1)
    page = kp_refs[0].shape[0] // n_heads
    q = q_ref[...]
    bias = bias_ref[...]
    ncol = q.shape[0]
    row = lax.broadcasted_iota(jnp.int32, (page, page), 0)
    col = lax.broadcasted_iota(jnp.int32, (page, page), 1)
    suffix_incl = (row >= col).astype(BF16)

    def head_rows(a, h):
        return a[h * SUBLANES:(h + 1) * SUBLANES]

    def weigh_head_major(a, val):
        cols = []
        for h in range(n_heads):
            parts = []
            if h > 0:
                parts.append(jnp.zeros((h * SUBLANES, page), F32))
            parts.append(head_rows(a, h))
            if h + 1 < n_heads:
                parts.append(jnp.zeros((ncol - (h + 1) * SUBLANES, page), F32))
            cols.append(jnp.concatenate(parts, axis=0))
        return _dot(jnp.concatenate(cols, axis=1).astype(BF16), val)

    def weigh_stored(a, val):
        spread = _dot(a, spread_ref[...]).astype(BF16) * same_head_ref[...]
        return _dot(spread, val)

    def weights(keys, valid, run):
        zt = _dot_nt(q, keys)
        z = jnp.concatenate([head_rows(zt, h)[:, h * page:(h + 1) * page] for h in range(n_heads)], axis=0) + bias
        sp = _softplus2(z)
        if valid is not None:
            sp = jnp.where(valid, sp, 0.0)
        incl = _split_dot(sp, suffix_incl, 2)
        a = jnp.exp2(z - incl - run)
        if valid is not None:
            a = jnp.where(valid, a, 0.0)
        return a, run + incl[:, 0:1]

    @pl.when(step == 0)
    def _():
        qry = lax.broadcasted_iota(jnp.int32, (ncol, page), 0) % SUBLANES
        key = lax.broadcasted_iota(jnp.int32, (ncol, page), 1)
        by_head = lambda r: jnp.concatenate([r[:, h * q.shape[1]:(h + 1) * q.shape[1]] for h in range(n_heads)], axis=0)
        a, run = weights(by_head(kn_ref[...]), (key < qry) & (qry < n_new), jnp.zeros((ncol, 1), F32))
        acc_ref[...] = weigh_head_major(a, by_head(vn_ref[...]))
        run_ref[...] = run

        @pl.when(pl.program_id(0) == 0)
        def _():
            probs_ref[...] = jnp.zeros_like(probs_ref)

    def keys_head_major(ref):
        heads = [ref[pl.ds(h, page, stride=n_heads), :] for h in range(n_heads)]
        return jnp.concatenate(heads, axis=0).astype(BF16)

    run = run_ref[...]
    total = jnp.zeros(acc_ref.shape, F32)
    for i in range(pages_per_step):
        total = total + weigh_stored(probs_ref[i], vp_refs[i][...].astype(BF16))
        a, run = weights(keys_head_major(kp_refs[i]), None, run)
        probs_ref[i] = a.astype(BF16)
    run_ref[...] = run
    acc_ref[...] += jnp.where(step > 0, total, 0.0)

    @pl.when(step == pl.num_programs(1) - 1)
    def _():
        o_ref[...] = acc_ref[...]


def _sb_sample(page_table, q, bias_cols, k_new, v_new, cache_k, cache_v, first_page, n_heads, n_new,
               pages_per_step):
    nb, n_pages = page_table.shape
    ncol, dh = q.shape[1], q.shape[2]
    w = n_heads * dh
    rows = cache_k.shape[1]
    page = rows // n_heads
    last = n_pages - 1

    n_steps = n_pages // pages_per_step

    def page_spec(i, lag):
        def index(b, p, pt):
            group = jnp.clip(p - lag, 0, n_steps - 1)
            return (pt[b, last - (group * pages_per_step + i)] + first_page, 0, 0)
        return pl.BlockSpec((None, rows, dh), index)

    c_tok = jnp.arange(rows)[None, :] // n_heads
    c_head = jnp.arange(rows)[None, :] % n_heads
    spread = (c_tok == jnp.arange(page)[:, None]).astype(BF16)
    same_head = (c_head == jnp.arange(ncol)[:, None] // SUBLANES).astype(BF16)
    const = lambda shape: pl.BlockSpec(shape, lambda b, p, pt: (0, 0))

    grid_spec = pltpu.PrefetchScalarGridSpec(
        num_scalar_prefetch=1,
        grid=(nb, n_steps + 1),
        in_specs=[
            pl.BlockSpec((None, ncol, dh), lambda b, p, pt: (b, 0, 0)),
            const((ncol, 1)),
            pl.BlockSpec((None, page, w), lambda b, p, pt: (b, 0, 0)),
            pl.BlockSpec((None, page, w), lambda b, p, pt: (b, 0, 0)),
            const((page, rows)),
            const((ncol, rows)),
        ] + [page_spec(i, 0) for i in range(pages_per_step)] + [page_spec(i, 1) for i in range(pages_per_step)],
        out_specs=pl.BlockSpec((None, ncol, dh), lambda b, p, pt: (b, 0, 0)),
        scratch_shapes=[pltpu.VMEM((ncol, dh), F32), pltpu.VMEM((ncol, 1), F32),
                        pltpu.VMEM((pages_per_step, ncol, page), BF16)],
    )
    return pl.pallas_call(
        functools.partial(_sb_sample_kernel, n_heads=n_heads, n_new=n_new,
                          pages_per_step=pages_per_step),
        out_shape=jax.ShapeDtypeStruct((nb, ncol, dh), F32),
        grid_spec=grid_spec,
        compiler_params=_params("arbitrary", "arbitrary"),
        name="sb_sample",
    )(page_table, q, bias_cols, k_new, v_new, spread, same_head,
      *([cache_k] * pages_per_step), *([cache_v] * pages_per_step))


def _mlstm_kernel(*refs, n_heads, n_valid, has_init):
    if has_init:
        (q_ref, k_ref, v_ref, o_ref, g_ref, bg_ref, gh_ref, c0_ref, n0_ref, m0_ref,
         y_ref, c_ref, n_ref, m_ref) = refs
    else:
        q_ref, k_ref, v_ref, o_ref, g_ref, bg_ref, gh_ref, y_ref, c_ref, n_ref, m_ref = refs
    chunk = pl.program_id(1)
    l = q_ref.shape[0]
    dk = q_ref.shape[1] // n_heads
    dv = v_ref.shape[1] // n_heads

    @pl.when(chunk == 0)
    def _():
        if has_init:
            c_ref[...] = c0_ref[...]
            n_ref[...] = n0_ref[...]
            m_ref[...] = m0_ref[...]
        else:
            c_ref[...] = jnp.zeros_like(c_ref)
            n_ref[...] = jnp.zeros_like(n_ref)
            m_ref[...] = jnp.zeros_like(m_ref)

    gates = g_ref[...] + bg_ref[...]
    lane = lax.broadcasted_iota(jnp.int32, gates.shape, 1)
    tok = lax.broadcasted_iota(jnp.int32, gates.shape, 0)
    is_forget = (lane >= n_heads) & (lane < 2 * n_heads)
    log_f = jnp.where(is_forget, -_softplus_log1p(-gates), 0.0)
    if n_valid < l:
        log_f = jnp.where(tok < n_valid, log_f, 0.0)
        gates = jnp.where(tok < n_valid, gates, -jnp.inf)
    ti = lax.broadcasted_iota(jnp.int32, (l, l), 0)
    si = lax.broadcasted_iota(jnp.int32, (l, l), 1)
    causal = si <= ti
    diag = si == ti
    tri = causal.astype(BF16)
    cum_col = _split_dot(log_f, tri, 3, left=True)

    def to_row(column):
        return jnp.sum(jnp.where(diag, column, 0.0), axis=0, keepdims=True)

    heads = range(n_heads)
    qs = [q_ref[:, h * dk:(h + 1) * dk] for h in heads]
    ks = [k_ref[:, h * dk:(h + 1) * dk] for h in heads]
    vs = [v_ref[:, h * dv:(h + 1) * dv] for h in heads]
    cs = [c_ref[h] for h in heads]
    ns = [n_ref[h:h + 1, :] for h in heads]
    ms = [m_ref[h:h + 1, 0:1] for h in heads]
    b_cols = [cum_col[:, n_heads + h:n_heads + h + 1] for h in heads]
    ig_cols = [gates[:, h:h + 1] for h in heads]

    qk = [_dot_nt(q, k) for q, k in zip(qs, ks)]
    qc = [_dot_nt(q, c.astype(BF16)) for q, c in zip(qs, cs)]

    ss, wis, m_ts = [], [], []
    for h in heads:
        d = jnp.where(causal, b_cols[h] - to_row(b_cols[h]) + to_row(ig_cols[h]), -jnp.inf)
        inter = b_cols[h] + ms[h]
        m_t = jnp.maximum(inter, jnp.max(d, axis=1, keepdims=True))
        ss.append(qk[h] * (jnp.exp(d - m_t) * dk ** -0.5))
        wis.append(jnp.exp(inter - m_t))
        m_ts.append(m_t)

    kws, w_cs = [], []
    for h in heads:
        b_last = b_cols[h][l - 1:l, :]
        g_col = b_last - b_cols[h] + ig_cols[h]
        m_new = jnp.maximum(b_last + ms[h], jnp.max(g_col, axis=0, keepdims=True))
        w_cs.append(jnp.exp(b_last + ms[h] - m_new))
        kws.append(jnp.exp(g_col - m_new) * (ks[h].astype(F32) * dk ** -0.5))
        m_ref[h:h + 1, :] = jnp.broadcast_to(m_new, (1, m_ref.shape[1]))

    sv = [_dot(s.astype(BF16), v) for s, v in zip(ss, vs)]
    vk = [_dot_tn(v, kw.astype(BF16)) for v, kw in zip(vs, kws)]

    for h in heads:
        c_ref[h] = w_cs[h] * cs[h] + vk[h]
        n_ref[h:h + 1, :] = w_cs[h] * ns[h] + jnp.sum(kws[h], axis=0, keepdims=True)
        num = wis[h] * qc[h] + sv[h]
        qn = jnp.sum(qs[h].astype(F32) * ns[h].astype(BF16).astype(F32), axis=1, keepdims=True)
        den = wis[h] * qn + jnp.sum(ss[h], axis=1, keepdims=True)
        hid = num / jnp.maximum(jnp.abs(den), jnp.exp(-m_ts[h]))
        hid = hid * lax.rsqrt(jnp.mean(hid * hid, axis=1, keepdims=True) + HEAD_EPS)
        gate = jax.nn.sigmoid(o_ref[:, h * dv:(h + 1) * dv])
        y_ref[:, h * dv:(h + 1) * dv] = (hid * gh_ref[:, h * dv:(h + 1) * dv] * gate).astype(BF16)


def _mlstm(q, k, v, o, gates, b_gate, g_head, n_heads, l, n_valid, init=None):
    nb, t, qk_w = q.shape
    v_w = v.shape[2]
    dk = qk_w // n_heads
    dv = v_w // n_heads
    tok = lambda width: pl.BlockSpec((None, l, width), lambda b, c: (b, c, 0))
    const = lambda shape: pl.BlockSpec(shape, lambda b, c: (0,) * len(shape))
    c_spec = pl.BlockSpec((None, n_heads, dv, dk), lambda b, c: (b, 0, 0, 0))
    n_spec = pl.BlockSpec((None, n_heads, dk), lambda b, c: (b, 0, 0))
    m_spec = pl.BlockSpec((None, n_heads, LANES), lambda b, c: (b, 0, 0))
    in_specs = [tok(qk_w), tok(qk_w), tok(v_w), tok(v_w), tok(LANES), const((1, LANES)), const((1, v_w))]
    args = [q, k, v, o, gates, b_gate, g_head]
    if init is not None:
        in_specs += [c_spec, n_spec, m_spec]
        args += list(init)
    return pl.pallas_call(
        functools.partial(_mlstm_kernel, n_heads=n_heads, n_valid=n_valid, has_init=init is not None),
        out_shape=(
            jax.ShapeDtypeStruct((nb, t, v_w), BF16),
            jax.ShapeDtypeStruct((nb, n_heads, dv, dk), F32),
            jax.ShapeDtypeStruct((nb, n_heads, dk), F32),
            jax.ShapeDtypeStruct((nb, n_heads, LANES), F32),
        ),
        grid=(nb, t // l),
        in_specs=in_specs,
        out_specs=[tok(v_w), c_spec, n_spec, m_spec],
        compiler_params=_params("arbitrary", "arbitrary"),
        name="mlstm",
    )(*args)


def _pick(n, candidates):
    for c in candidates:
        if n % c == 0:
            return c
    return n


def kernel(x_prompt, x_sample, cache_k, cache_v, page_table, state_C, state_n, state_m, state_conv, w_in_a, w_out_a, sb_bias, w_in_b, b_gate_b, g_head_b, w_out_b, w_up, conv_w, conv_b, w_down, ln1_g, ln1_b, ln2_g, ln2_b):
    depth = w_up.shape[0]
    alpha = (2 * depth) ** 0.25
    bp, seq, d_model = x_prompt.shape
    bs, t_new, _ = x_sample.shape
    n_heads_a = sb_bias.shape[1]
    w_a = w_in_a.shape[2] // 3
    dh_a = w_a // n_heads_a
    n_heads_b = state_C.shape[2]
    dv_b, dk_b = state_C.shape[3], state_C.shape[4]
    qk_w, v_w = n_heads_b * dk_b, n_heads_b * dv_b
    d_ff = w_up.shape[2] // 2
    page = cache_k.shape[2]
    assert bs == SUBLANES and t_new <= SUBLANES and w_a == d_model

    mp = bp * seq
    ms = bs * t_new
    tm_p = _pick(seq, TM_PROMPT)
    tn_f = _pick(d_ff, TN_FF)

    xp = x_prompt.reshape(mp, d_model)
    xs = x_sample.transpose(1, 0, 2).reshape(ms, d_model)

    def to_batch_major(a, pad_to=None):
        a = a.reshape(t_new, bs, -1).transpose(1, 0, 2)
        if pad_to is not None:
            a = jnp.pad(a, ((0, 0), (0, pad_to - t_new), (0, 0)))
        return a

    def to_time_major(a):
        return a[:, :t_new].transpose(1, 0, 2).reshape(ms, -1)

    outs_p = dict(k=[], v=[], c=[], n=[], m=[], conv=[])
    outs_s = dict(k=[], v=[], c=[], n=[], m=[], conv=[])
    xp_f, xs_f = xp, xs
    xp_b = xs_b = None

    w_in_a_b, w_out_a_b = w_in_a.astype(BF16), w_out_a.astype(BF16)
    w_in_b_b, w_out_b_b = w_in_b.astype(BF16), w_out_b.astype(BF16)
    w_down_b = w_down.astype(BF16)
    q_scale = dh_a ** -0.5 * LOG2_E

    for i in range(depth):
        j = i // N_MIXERS
        if i % N_MIXERS == 0:
            w_out = w_out_a_b
            qb, kb, vb, kf, vf = _qkv_proj(xp_f, w_in_a_b, j, q_scale, tm_p, TN_QKV)
            outs_p["k"].append(kf.reshape(bp, seq, n_heads_a, dh_a))
            outs_p["v"].append(vf.reshape(bp, seq, n_heads_a, dh_a))
            shp = (bp, seq, w_a)
            t_attn = _pick(seq, T_ATTN)
            n_sub = _pick(seq // t_attn, ATTN_SUBTILES)
            bias2 = sb_bias[j] * LOG2_E
            mix_p = _sb_prompt(qb.reshape(shp), kb.reshape(shp), vb.reshape(shp), bias2, n_heads_a,
                               t_attn, n_sub).reshape(mp, w_a)
            qb, kb, vb, kf, vf = _qkv_proj(xs_f, w_in_a_b, j, q_scale, ms, TN_QKV)
            outs_s["k"].append(to_batch_major(kf).reshape(bs, t_new, n_heads_a, dh_a))
            outs_s["v"].append(to_batch_major(vf).reshape(bs, t_new, n_heads_a, dh_a))
            q8 = to_batch_major(qb, SUBLANES).reshape(bs, SUBLANES, n_heads_a, dh_a)
            q_hq = q8.transpose(0, 2, 1, 3).reshape(bs, n_heads_a * SUBLANES, dh_a)
            bias_cols = jnp.repeat(bias2, SUBLANES)[:, None]
            n_layers_a, pool = cache_k.shape[0], cache_k.shape[1]
            as_stored = lambda c: c.reshape(n_layers_a * pool, page * n_heads_a, dh_a)
            o_s = _sb_sample(page_table, q_hq, bias_cols, to_batch_major(kb, page), to_batch_major(vb, page),
                             as_stored(cache_k), as_stored(cache_v), j * pool, n_heads_a, t_new,
                             _pick(page_table.shape[1], PAGES_PER_STEP))
            o_s = o_s.reshape(bs, n_heads_a, SUBLANES, dh_a).transpose(0, 2, 1, 3).reshape(bs, SUBLANES, w_a)
            mix_s = to_time_major(o_s).astype(BF16)
        else:
            w_out = w_out_b_b
            n_main = 2 * qk_w + 2 * v_w
            w_gate = jnp.pad(w_in_b_b[j, :, n_main:], ((0, 0), (0, LANES - 2 * n_heads_b)))
            b_gate = jnp.pad(b_gate_b[j], (0, LANES - 2 * n_heads_b))[None, :]
            g_head = g_head_b[j][None, :]
            q, k, v, o, g = _mlstm_proj(xp_b, w_in_b_b, j, w_gate, qk_w, v_w, tm_p, 4)
            r3 = lambda a: a.reshape(bp, seq, -1)
            l = MLSTM_CHUNK if seq % MLSTM_CHUNK == 0 else seq
            mix_p, c, n, m = _mlstm(r3(q), r3(k), r3(v), r3(o), r3(g), b_gate, g_head, n_heads_b, l, l)
            mix_p = mix_p.reshape(mp, v_w)
            outs_p["c"].append(c)
            outs_p["n"].append(n)
            outs_p["m"].append(m[:, :, 0])
            q, k, v, o, g = _mlstm_proj(xs_b, w_in_b_b, j, w_gate, qk_w, v_w, ms, 4)
            ls = MLSTM_CHUNK if t_new % MLSTM_CHUNK == 0 else t_new
            assert ls == t_new and t_new <= SAMPLE_CHUNK
            padl = lambda a: to_batch_major(a, SAMPLE_CHUNK)
            m0 = jnp.broadcast_to(state_m[j][:, :, None], (bs, n_heads_b, LANES))
            y, c, n, m = _mlstm(padl(q), padl(k), padl(v), padl(o), padl(g), b_gate, g_head, n_heads_b,
                                SAMPLE_CHUNK, t_new, init=(state_C[j], state_n[j], m0))
            mix_s = to_time_major(y)
            outs_s["c"].append(c)
            outs_s["n"].append(n)
            outs_s["m"].append(m[:, :, 0])

        g1, b1 = ln1_g[i][None, :], ln1_b[i][None, :]
        g2, b2 = ln2_g[i][None, :], ln2_b[i][None, :]
        cw, cb = conv_w[i], conv_b[i][None, :]

        xp_f, xp_b = _proj_ln(mix_p, w_out, j, xp_f, g1, b1, alpha, _pick(seq, TM_LN))
        tm_f = _pick(seq, TM_FFN)
        gp, conv_p = _ffn_up_prompt(xp_b, w_up, i, cw, cb, bp, tm_f, tn_f, _pick(tm_f, FFN_ROWS))
        outs_p["conv"].append(conv_p)
        xp_f, xp_b = _proj_ln(gp, w_down_b, i, xp_f, g2, b2, alpha, _pick(seq, TM_DOWN))

        xs_f, xs_b = _proj_ln(mix_s, w_out, j, xs_f, g1, b1, alpha, ms)
        gs, conv_s = _ffn_up_sample(xs_b, w_up, i, cw, cb, state_conv[i].transpose(1, 0, 2), t_new, tn_f)
        outs_s["conv"].append(conv_s.transpose(1, 0, 2))
        xs_f, xs_b = _proj_ln(gs, w_down_b, i, xs_f, g2, b2, alpha, ms)

    y_prompt = xp_f.reshape(bp, seq, d_model)
    y_sample = xs_f.reshape(t_new, bs, d_model).transpose(1, 0, 2)
    st = jnp.stack
    return (y_prompt, y_sample, st(outs_p["k"]), st(outs_p["v"]), st(outs_s["k"]), st(outs_s["v"]),
            st(outs_p["c"]), st(outs_p["n"]), st(outs_p["m"]), st(outs_s["c"]), st(outs_s["n"]), st(outs_s["m"]),
            st(outs_p["conv"]), st(outs_s["conv"]))
```

```python
import functools

import jax
import jax.numpy as jnp
from jax import lax
from jax.experimental import pallas as pl
from jax.experimental.pallas import tpu as pltpu

F32 = jnp.float32
BF16 = jnp.bfloat16

LN_EPS = 1e-5
HEAD_EPS = 1e-6
CONV_W = 3
MLSTM_CHUNK = 128
SAMPLE_CHUNK = 16
N_MIXERS = 2
LOG2_E = 1.4426950408889634

SUBLANES = 8
LANES = 128
MXU_DIM = 256
VMEM_LIMIT_BYTES = 56 * 1024 * 1024

TM_PROMPT = (1024, 512, 256, 128)
TM_FFN = (2048, 1024, 512, 256, 128)
TM_LN = (512, 256, 128)
TM_DOWN = (256, 128)
TN_FF = (512, 256, 128)
TN_QKV = 512
T_ATTN = (MXU_DIM, LANES)
FFN_ROWS = (512, 256, 128)
LN_ROWS = (256, 128)
ATTN_SUBTILES = (4, 2, 1)
ATTN_UNROLL = (4, 2, 1)
PAGES_PER_STEP = (8, 4, 2, 1)
MLSTM_GROUP = (2, 1)


def _params(*semantics):
    return pltpu.CompilerParams(dimension_semantics=semantics, vmem_limit_bytes=VMEM_LIMIT_BYTES)


def _dot(a, b):
    return jnp.dot(a, b, preferred_element_type=F32)


def _dot_nt(a, b):
    return lax.dot_general(a, b, (((1,), (1,)), ((), ())), preferred_element_type=F32)


def _dot_tn(a, b):
    return lax.dot_general(a, b, (((0,), (0,)), ((), ())), preferred_element_type=F32)


def _split_dot(x, t, n_terms, left=False):
    acc = None
    rem = x
    for i in range(n_terms):
        part = rem.astype(BF16)
        term = _dot(t, part) if left else _dot(part, t)
        acc = term if acc is None else acc + term
        if i + 1 < n_terms:
            rem = rem - part.astype(F32)
    return acc


def _softplus2(z):
    return jnp.maximum(z, 0.0) + jnp.log2(1.0 + jnp.exp2(-jnp.abs(z)))


def _softplus_log1p(z):
    return jnp.maximum(z, 0.0) + jnp.log1p(jnp.exp(-jnp.abs(z)))


def _layer_norm_rows(z, g, b):
    mu = jnp.mean(z, axis=-1, keepdims=True)
    zc = z - mu
    var = jnp.mean(zc * zc, axis=-1, keepdims=True)
    return zc * lax.rsqrt(var + LN_EPS) * g + b


def _qkv_kernel(x_ref, wq_ref, wk_ref, wv_ref, qb_ref, kb_ref, vb_ref, kf_ref, vf_ref, xb_ref, *, q_scale):
    @pl.when(pl.program_id(1) == 0)
    def _():
        xb_ref[...] = x_ref[...].astype(BF16)

    xb = xb_ref[...]
    qb_ref[...] = (_dot(xb, wq_ref[...]) * q_scale).astype(BF16)
    k = _dot(xb, wk_ref[...])
    kf_ref[...] = k
    kb_ref[...] = k.astype(BF16)
    v = _dot(xb, wv_ref[...])
    vf_ref[...] = v
    vb_ref[...] = v.astype(BF16)


def _qkv_proj(x, w, layer, q_scale, tm, tn):
    m, d = x.shape
    wa = w.shape[2] // 3
    nj = wa // tn
    row = lambda i, j: (i, j)
    out_bf = jax.ShapeDtypeStruct((m, wa), BF16)
    out_f = jax.ShapeDtypeStruct((m, wa), F32)
    return pl.pallas_call(
        functools.partial(_qkv_kernel, q_scale=q_scale),
        out_shape=(out_bf, out_bf, out_bf, out_f, out_f),
        grid=(m // tm, nj),
        in_specs=[
            pl.BlockSpec((tm, d), lambda i, j: (i, 0)),
            pl.BlockSpec((None, d, tn), lambda i, j: (layer, 0, j)),
            pl.BlockSpec((None, d, tn), lambda i, j: (layer, 0, nj + j)),
            pl.BlockSpec((None, d, tn), lambda i, j: (layer, 0, 2 * nj + j)),
        ],
        out_specs=[pl.BlockSpec((tm, tn), row)] * 5,
        scratch_shapes=[pltpu.VMEM((tm, d), BF16)],
        compiler_params=_params("arbitrary", "arbitrary"),
        name="qkv_proj",
    )(x, w, w, w)


def _mlstm_proj_kernel(x_ref, wq_ref, wk_ref, wv_ref, wo_ref, wg_ref, q_ref, k_ref, v_ref, o_ref, g_ref):
    xb = x_ref[...]
    q_ref[...] = _dot(xb, wq_ref[...]).astype(BF16)
    k_ref[...] = _dot(xb, wk_ref[...]).astype(BF16)
    v_ref[...] = _dot(xb, wv_ref[...]).astype(BF16)
    o_ref[...] = _dot(xb, wo_ref[...])

    @pl.when(pl.program_id(1) == 0)
    def _():
        g_ref[...] = _dot(xb, wg_ref[...])


def _mlstm_proj(xb, w, layer, w_gate, qk_w, v_w, tm, nj):
    m, d = xb.shape
    tq = qk_w // nj
    tv = v_w // nj
    return pl.pallas_call(
        _mlstm_proj_kernel,
        out_shape=(
            jax.ShapeDtypeStruct((m, qk_w), BF16),
            jax.ShapeDtypeStruct((m, qk_w), BF16),
            jax.ShapeDtypeStruct((m, v_w), BF16),
            jax.ShapeDtypeStruct((m, v_w), F32),
            jax.ShapeDtypeStruct((m, LANES), F32),
        ),
        grid=(m // tm, nj),
        in_specs=[
            pl.BlockSpec((tm, d), lambda i, j: (i, 0)),
            pl.BlockSpec((None, d, tq), lambda i, j: (layer, 0, j)),
            pl.BlockSpec((None, d, tq), lambda i, j: (layer, 0, nj + j)),
            pl.BlockSpec((None, d, tv), lambda i, j: (layer, 0, (2 * qk_w) // tv + j)),
            pl.BlockSpec((None, d, tv), lambda i, j: (layer, 0, (2 * qk_w + v_w) // tv + j)),
            pl.BlockSpec((d, LANES), lambda i, j: (0, 0)),
        ],
        out_specs=[
            pl.BlockSpec((tm, tq), lambda i, j: (i, j)),
            pl.BlockSpec((tm, tq), lambda i, j: (i, j)),
            pl.BlockSpec((tm, tv), lambda i, j: (i, j)),
            pl.BlockSpec((tm, tv), lambda i, j: (i, j)),
            pl.BlockSpec((tm, LANES), lambda i, j: (i, 0)),
        ],
        compiler_params=_params("arbitrary", "arbitrary"),
        name="mlstm_proj",
    )(xb, w, w, w, w, w_gate)


def _proj_ln_kernel(a_ref, w_ref, r_ref, g_ref, b_ref, of_ref, ob_ref, *, alpha, rows):
    for r in range(0, a_ref.shape[0], rows):
        y = _dot(a_ref[r:r + rows, :], w_ref[...])
        out = _layer_norm_rows(alpha * r_ref[r:r + rows, :] + y, g_ref[...], b_ref[...])
        of_ref[r:r + rows, :] = out
        ob_ref[r:r + rows, :] = out.astype(BF16)


def _proj_ln(a, w, layer, resid, g, b, alpha, tm):
    m, kd = a.shape
    d = w.shape[2]
    return pl.pallas_call(
        functools.partial(_proj_ln_kernel, alpha=alpha, rows=_pick(tm, LN_ROWS)),
        out_shape=(jax.ShapeDtypeStruct((m, d), F32), jax.ShapeDtypeStruct((m, d), BF16)),
        grid=(m // tm,),
        in_specs=[
            pl.BlockSpec((tm, kd), lambda i: (i, 0)),
            pl.BlockSpec((None, kd, d), lambda i: (layer, 0, 0), pipeline_mode=pl.Buffered(1)),
            pl.BlockSpec((tm, d), lambda i: (i, 0)),
            pl.BlockSpec((1, d), lambda i: (0, 0)),
            pl.BlockSpec((1, d), lambda i: (0, 0)),
        ],
        out_specs=[pl.BlockSpec((tm, d), lambda i: (i, 0))] * 2,
        compiler_params=_params("arbitrary"),
        name="proj_ln",
    )(a, w, resid, g, b)


def _silu_gate(cg, cv):
    return (cg * jax.nn.sigmoid(cg) * cv).astype(BF16)


def _ffn_up_prompt_kernel(x_ref, wg_ref, wv_ref, cwg_ref, cwv_ref, cbg_ref, cbv_ref,
                          o_ref, sg_ref, sv_ref, wgb_ref, wvb_ref, carry_ref, *, tiles_per_seq, rows):
    i = pl.program_id(1)
    tm = x_ref.shape[0]
    tn = o_ref.shape[1]

    @pl.when(i == 0)
    def _():
        wgb_ref[...] = wg_ref[...].astype(BF16)
        wvb_ref[...] = wv_ref[...].astype(BF16)

    @pl.when((i % tiles_per_seq) == 0)
    def _():
        carry_ref[...] = jnp.zeros_like(carry_ref)

    rowi = lax.broadcasted_iota(jnp.int32, (SUBLANES, tn), 0)

    def conv(u, prev, cw_ref, cb_ref):
        w0 = cw_ref[0:1, :]
        w1 = cw_ref[1:2, :]
        w2 = cw_ref[2:3, :]
        bias = cb_ref[...]
        c = bias + w0 * pltpu.roll(u, 2, 0) + w1 * pltpu.roll(u, 1, 0) + w2 * u
        u8 = u[:SUBLANES, :]
        p0 = prev[SUBLANES - 2:SUBLANES - 1, :]
        p1 = prev[SUBLANES - 1:SUBLANES, :]
        m1 = jnp.where(rowi == 0, p1, pltpu.roll(u8, 1, 0))
        m2 = jnp.where(rowi == 0, p0, jnp.where(rowi == 1, p1, pltpu.roll(u8, 2, 0)))
        c8 = bias + w0 * m2 + w1 * m1 + w2 * u8
        return jnp.concatenate([c8, c[SUBLANES:, :]], axis=0)

    prev_g = carry_ref[0]
    prev_v = carry_ref[1]
    for r in range(0, tm, rows):
        xb = x_ref[r:r + rows, :]
        ug = _dot(xb, wgb_ref[...])
        uv = _dot(xb, wvb_ref[...])
        o_ref[r:r + rows, :] = _silu_gate(conv(ug, prev_g, cwg_ref, cbg_ref), conv(uv, prev_v, cwv_ref, cbv_ref))
        prev_g = ug[rows - SUBLANES:, :]
        prev_v = uv[rows - SUBLANES:, :]
    carry_ref[0] = prev_g
    carry_ref[1] = prev_v
    sg_ref[...] = prev_g[SUBLANES - (CONV_W - 1):, :]
    sv_ref[...] = prev_v[SUBLANES - (CONV_W - 1):, :]


def _ffn_up_prompt(xb, w_up, layer, conv_w, conv_b, batch, tm, tn, rows):
    m, d = xb.shape
    f = w_up.shape[2] // 2
    nj = f // tn
    seq = m // batch
    tiles_per_seq = seq // tm
    state = jax.ShapeDtypeStruct((m // tm, CONV_W - 1, f), F32)
    g, sg, sv = pl.pallas_call(
        functools.partial(_ffn_up_prompt_kernel, tiles_per_seq=tiles_per_seq, rows=rows),
        out_shape=(jax.ShapeDtypeStruct((m, f), BF16), state, state),
        grid=(nj, m // tm),
        in_specs=[
            pl.BlockSpec((tm, d), lambda j, i: (i, 0)),
            pl.BlockSpec((None, d, tn), lambda j, i: (layer, 0, j)),
            pl.BlockSpec((None, d, tn), lambda j, i: (layer, 0, nj + j)),
            pl.BlockSpec((CONV_W, tn), lambda j, i: (0, j)),
            pl.BlockSpec((CONV_W, tn), lambda j, i: (0, nj + j)),
            pl.BlockSpec((1, tn), lambda j, i: (0, j)),
            pl.BlockSpec((1, tn), lambda j, i: (0, nj + j)),
        ],
        out_specs=[
            pl.BlockSpec((tm, tn), lambda j, i: (i, j)),
            pl.BlockSpec((None, CONV_W - 1, tn), lambda j, i: (i, 0, j)),
            pl.BlockSpec((None, CONV_W - 1, tn), lambda j, i: (i, 0, j)),
        ],
        scratch_shapes=[pltpu.VMEM((d, tn), BF16), pltpu.VMEM((d, tn), BF16),
                        pltpu.VMEM((2, SUBLANES, tn), F32)],
        compiler_params=_params("arbitrary", "arbitrary"),
        name="ffn_up_prompt",
    )(xb, w_up, w_up, conv_w, conv_w, conv_b, conv_b)
    last = slice(tiles_per_seq - 1, None, tiles_per_seq)
    return g, jnp.concatenate([sg[last], sv[last]], axis=-1)


def _ffn_up_sample_kernel(x_ref, wg_ref, wv_ref, cwg_ref, cwv_ref, cbg_ref, cbv_ref, pg_ref, pv_ref,
                          o_ref, sg_ref, sv_ref, *, n_t, nb):
    xb = x_ref[...]

    def conv(u, cw_ref, cb_ref, p_ref, s_ref):
        up = [p_ref[0], p_ref[1]] + [u[t * nb:(t + 1) * nb, :] for t in range(n_t)]
        s_ref[0] = up[n_t]
        s_ref[1] = up[n_t + 1]
        bias = cb_ref[...]
        return [bias + cw_ref[0:1, :] * up[t] + cw_ref[1:2, :] * up[t + 1] + cw_ref[2:3, :] * up[t + 2]
                for t in range(n_t)]

    cg = conv(_dot(xb, wg_ref[...].astype(BF16)), cwg_ref, cbg_ref, pg_ref, sg_ref)
    cv = conv(_dot(xb, wv_ref[...].astype(BF16)), cwv_ref, cbv_ref, pv_ref, sv_ref)
    for t in range(n_t):
        o_ref[t * nb:(t + 1) * nb, :] = _silu_gate(cg[t], cv[t])


def _ffn_up_sample(xb, w_up, layer, conv_w, conv_b, prefix, n_t, tn):
    m, d = xb.shape
    nb = m // n_t
    f = w_up.shape[2] // 2
    nj = f // tn
    state = jax.ShapeDtypeStruct((CONV_W - 1, nb, f), F32)
    g, sg, sv = pl.pallas_call(
        functools.partial(_ffn_up_sample_kernel, n_t=n_t, nb=nb),
        out_shape=(jax.ShapeDtypeStruct((m, f), BF16), state, state),
        grid=(nj,),
        in_specs=[
            pl.BlockSpec((m, d), lambda j: (0, 0)),
            pl.BlockSpec((None, d, tn), lambda j: (layer, 0, j)),
            pl.BlockSpec((None, d, tn), lambda j: (layer, 0, nj + j)),
            pl.BlockSpec((CONV_W, tn), lambda j: (0, j)),
            pl.BlockSpec((CONV_W, tn), lambda j: (0, nj + j)),
            pl.BlockSpec((1, tn), lambda j: (0, j)),
            pl.BlockSpec((1, tn), lambda j: (0, nj + j)),
            pl.BlockSpec((CONV_W - 1, nb, tn), lambda j: (0, 0, j)),
            pl.BlockSpec((CONV_W - 1, nb, tn), lambda j: (0, 0, nj + j)),
        ],
        out_specs=[
            pl.BlockSpec((m, tn), lambda j: (0, j)),
            pl.BlockSpec((CONV_W - 1, nb, tn), lambda j: (0, 0, j)),
            pl.BlockSpec((CONV_W - 1, nb, tn), lambda j: (0, 0, j)),
        ],
        compiler_params=_params("arbitrary"),
        name="ffn_up_sample",
    )(xb, w_up, w_up, conv_w, conv_w, conv_b, conv_b, prefix, prefix)
    return g, jnp.concatenate([sg, sv], axis=-1)


def _sb_prompt_kernel(bias_ref, q_ref, k_ref, v_ref, o_ref, acc_ref, run_ref, *, t, n_sub, unroll):
    h = pl.program_id(1)
    qi = pl.program_id(2)
    bias = bias_ref[h]
    row = lax.broadcasted_iota(jnp.int32, (t, t), 0)
    col = lax.broadcasted_iota(jnp.int32, (t, t), 1)
    suffix_incl = (row >= col).astype(BF16)
    valid = col < row

    def process(key_blocks, pairs):
        starts = [pl.multiple_of(j * t, t) for j in key_blocks]
        ks = [k_ref[pl.ds(st, t), :] for st in starts]
        vs = [v_ref[pl.ds(st, t), :] for st in starts]
        zs = [_dot_nt(q_ref[s * t:(s + 1) * t, :], ks[kb]) + bias for kb, s, _ in pairs]
        sps = [_softplus2(z) for z in zs]
        sps = [jnp.where(valid, sp, 0.0) if diagonal else sp for sp, (_, _, diagonal) in zip(sps, pairs)]
        incls = [_dot(sp.astype(BF16), suffix_incl) for sp in sps]
        runs = {s: run_ref[s] for s in sorted({s for _, s, _ in pairs})}
        probs = []
        for (_, s, diagonal), z, incl in zip(pairs, zs, incls):
            a = jnp.exp2(z - incl - runs[s])
            probs.append((jnp.where(valid, a, 0.0) if diagonal else a).astype(BF16))
            runs[s] = runs[s] + incl[:, 0:1]
        outs = [_dot(a, vs[kb]) for a, (kb, _, _) in zip(probs, pairs)]
        for s, run in runs.items():
            run_ref[s] = run
            acc_ref[s] += functools.reduce(lambda x, y: x + y, [o for o, (_, s2, _) in zip(outs, pairs) if s2 == s])

    top = [n_sub * qi + r for r in range(n_sub - 1, -1, -1)]
    acc_ref[...] = jnp.zeros_like(acc_ref)
    run_ref[...] = jnp.zeros_like(run_ref)
    process(top, [(slot, s, s == n_sub - 1 - slot) for slot in range(n_sub) for s in range(n_sub - 1 - slot, n_sub)])

    def body(it, carry):
        first = n_sub * qi - 1 - it * unroll
        process([first - u for u in range(unroll)], [(u, s, False) for u in range(unroll) for s in range(n_sub)])
        return carry

    lax.fori_loop(0, (n_sub * qi) // unroll, body, 0)
    for s in range(n_sub):
        o_ref[s * t:(s + 1) * t, :] = acc_ref[s].astype(BF16)


def _sb_prompt(q, k, v, bias, n_heads, t, n_sub):
    b, s, w = q.shape
    dh = w // n_heads
    tq = t * n_sub
    unroll = _pick(n_sub, ATTN_UNROLL)
    return pl.pallas_call(
        functools.partial(_sb_prompt_kernel, t=t, n_sub=n_sub, unroll=unroll),
        out_shape=jax.ShapeDtypeStruct((b, s, w), BF16),
        grid=(b, n_heads, s // tq),
        in_specs=[
            pl.BlockSpec(memory_space=pltpu.SMEM),
            pl.BlockSpec((None, tq, dh), lambda bi, h, qi: (bi, qi, h)),
            pl.BlockSpec((None, s, dh), lambda bi, h, qi: (bi, 0, h)),
            pl.BlockSpec((None, s, dh), lambda bi, h, qi: (bi, 0, h)),
        ],
        out_specs=pl.BlockSpec((None, tq, dh), lambda bi, h, qi: (bi, qi, h)),
        scratch_shapes=[pltpu.VMEM((n_sub, t, dh), F32), pltpu.VMEM((n_sub, t, 1), F32)],
        compiler_params=_params("arbitrary", "arbitrary", "arbitrary"),
        name="sb_prompt",
    )(bias, q, k, v)


def _sb_sample_kernel(pt_ref, q_ref, bias_ref, kn_ref, vn_ref, spread_ref, same_head_ref, *refs,
                      n_heads, n_new, pages_per_step):
    kp_refs = refs[:pages_per_step]
    vp_refs = refs[pages_per_step:2 * pages_per_step]
    o_ref, acc_ref, run_ref, probs_ref = refs[2 * pages_per_step:]
    step = pl.program_id(1)
    page = kp_refs[0].shape[0] // n_heads
    q = q_ref[...]
    bias = bias_ref[...]
    ncol = q.shape[0]
    row = lax.broadcasted_iota(jnp.int32, (page, page), 0)
    col = lax.broadcasted_iota(jnp.int32, (page, page), 1)
    suffix_incl = (row >= col).astype(BF16)

    def head_rows(a, h):
        return a[h * SUBLANES:(h + 1) * SUBLANES]

    def weigh_head_major(a, val):
        cols = []
        for h in range(n_heads):
            parts = []
            if h > 0:
                parts.append(jnp.zeros((h * SUBLANES, page), F32))
            parts.append(head_rows(a, h))
            if h + 1 < n_heads:
                parts.append(jnp.zeros((ncol - (h + 1) * SUBLANES, page), F32))
            cols.append(jnp.concatenate(parts, axis=0))
        return _dot(jnp.concatenate(cols, axis=1).astype(BF16), val)

    def weigh_stored(a, val):
        spread = _dot(a, spread_ref[...]).astype(BF16) * same_head_ref[...]
        return _dot(spread, val)

    def weights(keys, valid, run):
        zt = _dot_nt(q, keys)
        z = jnp.concatenate([head_rows(zt, h)[:, h * page:(h + 1) * page] for h in range(n_heads)], axis=0) + bias
        sp = _softplus2(z)
        if valid is not None:
            sp = jnp.where(valid, sp, 0.0)
        incl = _split_dot(sp, suffix_incl, 2)
        a = jnp.exp2(z - incl - run)
        if valid is not None:
            a = jnp.where(valid, a, 0.0)
        return a, run + incl[:, 0:1]

    @pl.when(step == 0)
    def _():
        qry = lax.broadcasted_iota(jnp.int32, (ncol, page), 0) % SUBLANES
        key = lax.broadcasted_iota(jnp.int32, (ncol, page), 1)
        by_head = lambda r: jnp.concatenate([r[:, h * q.shape[1]:(h + 1) * q.shape[1]] for h in range(n_heads)], axis=0)
        a, run = weights(by_head(kn_ref[...]), (key < qry) & (qry < n_new), jnp.zeros((ncol, 1), F32))
        acc_ref[...] = weigh_head_major(a, by_head(vn_ref[...]))
        run_ref[...] = run

        @pl.when(pl.program_id(0) == 0)
        def _():
            probs_ref[...] = jnp.zeros_like(probs_ref)

    def keys_head_major(ref):
        heads = [ref[pl.ds(h, page, stride=n_heads), :] for h in range(n_heads)]
        return jnp.concatenate(heads, axis=0).astype(BF16)

    run = run_ref[...]
    total = jnp.zeros(acc_ref.shape, F32)
    for i in range(pages_per_step):
        total = total + weigh_stored(probs_ref[i], vp_refs[i][...].astype(BF16))
        a, run = weights(keys_head_major(kp_refs[i]), None, run)
        probs_ref[i] = a.astype(BF16)
    run_ref[...] = run
    acc_ref[...] += jnp.where(step > 0, total, 0.0)

    @pl.when(step == pl.num_programs(1) - 1)
    def _():
        o_ref[...] = acc_ref[...]


def _sb_sample(page_table, q, bias_cols, k_new, v_new, cache_k, cache_v, first_page, n_heads, n_new,
               pages_per_step):
    nb, n_pages = page_table.shape
    ncol, dh = q.shape[1], q.shape[2]
    w = n_heads * dh
    rows = cache_k.shape[1]
    page = rows // n_heads
    last = n_pages - 1

    n_steps = n_pages // pages_per_step

    def page_spec(i, lag):
        def index(b, p, pt):
            group = jnp.clip(p - lag, 0, n_steps - 1)
            return (pt[b, last - (group * pages_per_step + i)] + first_page, 0, 0)
        return pl.BlockSpec((None, rows, dh), index)

    c_tok = jnp.arange(rows)[None, :] // n_heads
    c_head = jnp.arange(rows)[None, :] % n_heads
    spread = (c_tok == jnp.arange(page)[:, None]).astype(BF16)
    same_head = (c_head == jnp.arange(ncol)[:, None] // SUBLANES).astype(BF16)
    const = lambda shape: pl.BlockSpec(shape, lambda b, p, pt: (0, 0))

    grid_spec = pltpu.PrefetchScalarGridSpec(
        num_scalar_prefetch=1,
        grid=(nb, n_steps + 1),
        in_specs=[
            pl.BlockSpec((None, ncol, dh), lambda b, p, pt: (b, 0, 0)),
            const((ncol, 1)),
            pl.BlockSpec((None, page, w), lambda b, p, pt: (b, 0, 0)),
            pl.BlockSpec((None, page, w), lambda b, p, pt: (b, 0, 0)),
            const((page, rows)),
            const((ncol, rows)),
        ] + [page_spec(i, 0) for i in range(pages_per_step)] + [page_spec(i, 1) for i in range(pages_per_step)],
        out_specs=pl.BlockSpec((None, ncol, dh), lambda b, p, pt: (b, 0, 0)),
        scratch_shapes=[pltpu.VMEM((ncol, dh), F32), pltpu.VMEM((ncol, 1), F32),
                        pltpu.VMEM((pages_per_step, ncol, page), BF16)],
    )
    return pl.pallas_call(
        functools.partial(_sb_sample_kernel, n_heads=n_heads, n_new=n_new,
                          pages_per_step=pages_per_step),
        out_shape=jax.ShapeDtypeStruct((nb, ncol, dh), F32),
        grid_spec=grid_spec,
        compiler_params=_params("arbitrary", "arbitrary"),
        name="sb_sample",
    )(page_table, q, bias_cols, k_new, v_new, spread, same_head,
      *([cache_k] * pages_per_step), *([cache_v] * pages_per_step))


def _mlstm_kernel(*refs, n_heads, n_valid, has_init):
    if has_init:
        (q_ref, k_ref, v_ref, o_ref, g_ref, bg_ref, gh_ref, c0_ref, n0_ref, m0_ref,
         y_ref, c_ref, n_ref, m_ref) = refs
    else:
        q_ref, k_ref, v_ref, o_ref, g_ref, bg_ref, gh_ref, y_ref, c_ref, n_ref, m_ref = refs
    chunk = pl.program_id(1)
    group, l = q_ref.shape[0], q_ref.shape[1]
    dk = q_ref.shape[2] // n_heads
    dv = v_ref.shape[2] // n_heads

    @pl.when(chunk == 0)
    def _():
        if has_init:
            c_ref[...] = c0_ref[...]
            n_ref[...] = n0_ref[...]
            m_ref[...] = m0_ref[...]
        else:
            c_ref[...] = jnp.zeros_like(c_ref)
            n_ref[...] = jnp.zeros_like(n_ref)
            m_ref[...] = jnp.zeros_like(m_ref)

    lane = lax.broadcasted_iota(jnp.int32, (l, LANES), 1)
    tok = lax.broadcasted_iota(jnp.int32, (l, LANES), 0)
    is_forget = (lane >= n_heads) & (lane < 2 * n_heads)
    ti = lax.broadcasted_iota(jnp.int32, (l, l), 0)
    si = lax.broadcasted_iota(jnp.int32, (l, l), 1)
    causal = si <= ti
    diag = si == ti
    tri = causal.astype(BF16)
    gates_all, cum_cols = [], []
    for b in range(group):
        gates = g_ref[b] + bg_ref[...]
        log_f = jnp.where(is_forget, -_softplus_log1p(-gates), 0.0)
        if n_valid < l:
            log_f = jnp.where(tok < n_valid, log_f, 0.0)
            gates = jnp.where(tok < n_valid, gates, -jnp.inf)
        gates_all.append(gates)
        cum_cols.append(_split_dot(log_f, tri, 3, left=True))

    def to_row(column):
        return jnp.sum(jnp.where(diag, column, 0.0), axis=0, keepdims=True)

    chains = [(b, h) for b in range(group) for h in range(n_heads)]
    idx = range(len(chains))
    qs = [q_ref[b, :, h * dk:(h + 1) * dk] for b, h in chains]
    ks = [k_ref[b, :, h * dk:(h + 1) * dk] for b, h in chains]
    vs = [v_ref[b, :, h * dv:(h + 1) * dv] for b, h in chains]
    cs = [c_ref[b, h] for b, h in chains]
    ns = [n_ref[b, h:h + 1, :] for b, h in chains]
    ms = [m_ref[b, h:h + 1, 0:1] for b, h in chains]
    b_cols = [cum_cols[b][:, n_heads + h:n_heads + h + 1] for b, h in chains]
    ig_cols = [gates_all[b][:, h:h + 1] for b, h in chains]

    qk = [_dot_nt(q, k) for q, k in zip(qs, ks)]
    qc = [_dot_nt(q, c.astype(BF16)) for q, c in zip(qs, cs)]

    ss, wis, m_ts = [], [], []
    for i in idx:
        d = jnp.where(causal, b_cols[i] - to_row(b_cols[i]) + to_row(ig_cols[i]), -jnp.inf)
        inter = b_cols[i] + ms[i]
        m_t = jnp.maximum(inter, jnp.max(d, axis=1, keepdims=True))
        ss.append(qk[i] * (jnp.exp(d - m_t) * dk ** -0.5))
        wis.append(jnp.exp(inter - m_t))
        m_ts.append(m_t)

    kws, w_cs = [], []
    for i, (b, h) in enumerate(chains):
        b_last = b_cols[i][l - 1:l, :]
        g_col = b_last - b_cols[i] + ig_cols[i]
        m_new = jnp.maximum(b_last + ms[i], jnp.max(g_col, axis=0, keepdims=True))
        w_cs.append(jnp.exp(b_last + ms[i] - m_new))
        kws.append(jnp.exp(g_col - m_new) * (ks[i].astype(F32) * dk ** -0.5))
        m_ref[b, h:h + 1, :] = jnp.broadcast_to(m_new, (1, m_ref.shape[2]))

    sv = [_dot(s.astype(BF16), v) for s, v in zip(ss, vs)]
    vk = [_dot_tn(v, kw.astype(BF16)) for v, kw in zip(vs, kws)]

    for i, (b, h) in enumerate(chains):
        c_ref[b, h] = w_cs[i] * cs[i] + vk[i]
        n_ref[b, h:h + 1, :] = w_cs[i] * ns[i] + jnp.sum(kws[i], axis=0, keepdims=True)
        num = wis[i] * qc[i] + sv[i]
        qn = jnp.sum(qs[i].astype(F32) * ns[i].astype(BF16).astype(F32), axis=1, keepdims=True)
        den = wis[i] * qn + jnp.sum(ss[i], axis=1, keepdims=True)
        hid = num / jnp.maximum(jnp.abs(den), jnp.exp(-m_ts[i]))
        hid = hid * lax.rsqrt(jnp.mean(hid * hid, axis=1, keepdims=True) + HEAD_EPS)
        gate = jax.nn.sigmoid(o_ref[b, :, h * dv:(h + 1) * dv])
        y_ref[b, :, h * dv:(h + 1) * dv] = (hid * gh_ref[:, h * dv:(h + 1) * dv] * gate).astype(BF16)


def _mlstm(q, k, v, o, gates, b_gate, g_head, n_heads, l, n_valid, init=None):
    nb, t, qk_w = q.shape
    v_w = v.shape[2]
    dk = qk_w // n_heads
    dv = v_w // n_heads
    group = _pick(nb, MLSTM_GROUP)
    tok = lambda width: pl.BlockSpec((group, l, width), lambda b, c: (b, c, 0))
    const = lambda shape: pl.BlockSpec(shape, lambda b, c: (0,) * len(shape))
    c_spec = pl.BlockSpec((group, n_heads, dv, dk), lambda b, c: (b, 0, 0, 0))
    n_spec = pl.BlockSpec((group, n_heads, dk), lambda b, c: (b, 0, 0))
    m_spec = pl.BlockSpec((group, n_heads, LANES), lambda b, c: (b, 0, 0))
    in_specs = [tok(qk_w), tok(qk_w), tok(v_w), tok(v_w), tok(LANES), const((1, LANES)), const((1, v_w))]
    args = [q, k, v, o, gates, b_gate, g_head]
    if init is not None:
        in_specs += [c_spec, n_spec, m_spec]
        args += list(init)
    return pl.pallas_call(
        functools.partial(_mlstm_kernel, n_heads=n_heads, n_valid=n_valid, has_init=init is not None),
        out_shape=(
            jax.ShapeDtypeStruct((nb, t, v_w), BF16),
            jax.ShapeDtypeStruct((nb, n_heads, dv, dk), F32),
            jax.ShapeDtypeStruct((nb, n_heads, dk), F32),
            jax.ShapeDtypeStruct((nb, n_heads, LANES), F32),
        ),
        grid=(nb // group, t // l),
        in_specs=in_specs,
        out_specs=[tok(v_w), c_spec, n_spec, m_spec],
        compiler_params=_params("arbitrary", "arbitrary"),
        name="mlstm",
    )(*args)


def _pick(n, candidates):
    for c in candidates:
        if n % c == 0:
            return c
    return n


def kernel(x_prompt, x_sample, cache_k, cache_v, page_table, state_C, state_n, state_m, state_conv, w_in_a, w_out_a, sb_bias, w_in_b, b_gate_b, g_head_b, w_out_b, w_up, conv_w, conv_b, w_down, ln1_g, ln1_b, ln2_g, ln2_b):
    depth = w_up.shape[0]
    alpha = (2 * depth) ** 0.25
    bp, seq, d_model = x_prompt.shape
    bs, t_new, _ = x_sample.shape
    n_heads_a = sb_bias.shape[1]
    w_a = w_in_a.shape[2] // 3
    dh_a = w_a // n_heads_a
    n_heads_b = state_C.shape[2]
    dv_b, dk_b = state_C.shape[3], state_C.shape[4]
    qk_w, v_w = n_heads_b * dk_b, n_heads_b * dv_b
    d_ff = w_up.shape[2] // 2
    page = cache_k.shape[2]
    assert bs == SUBLANES and t_new <= SUBLANES and w_a == d_model

    mp = bp * seq
    ms = bs * t_new
    tm_p = _pick(seq, TM_PROMPT)
    tn_f = _pick(d_ff, TN_FF)

    xp = x_prompt.reshape(mp, d_model)
    xs = x_sample.transpose(1, 0, 2).reshape(ms, d_model)

    def to_batch_major(a, pad_to=None):
        a = a.reshape(t_new, bs, -1).transpose(1, 0, 2)
        if pad_to is not None:
            a = jnp.pad(a, ((0, 0), (0, pad_to - t_new), (0, 0)))
        return a

    def to_time_major(a):
        return a[:, :t_new].transpose(1, 0, 2).reshape(ms, -1)

    outs_p = dict(k=[], v=[], c=[], n=[], m=[], conv=[])
    outs_s = dict(k=[], v=[], c=[], n=[], m=[], conv=[])
    xp_f, xs_f = xp, xs
    xp_b = xs_b = None

    w_in_a_b, w_out_a_b = w_in_a.astype(BF16), w_out_a.astype(BF16)
    w_in_b_b, w_out_b_b = w_in_b.astype(BF16), w_out_b.astype(BF16)
    w_down_b = w_down.astype(BF16)
    q_scale = dh_a ** -0.5 * LOG2_E

    for i in range(depth):
        j = i // N_MIXERS
        if i % N_MIXERS == 0:
            w_out = w_out_a_b
            qb, kb, vb, kf, vf = _qkv_proj(xp_f, w_in_a_b, j, q_scale, tm_p, TN_QKV)
            outs_p["k"].append(kf.reshape(bp, seq, n_heads_a, dh_a))
            outs_p["v"].append(vf.reshape(bp, seq, n_heads_a, dh_a))
            shp = (bp, seq, w_a)
            t_attn = _pick(seq, T_ATTN)
            n_sub = _pick(seq // t_attn, ATTN_SUBTILES)
            bias2 = sb_bias[j] * LOG2_E
            mix_p = _sb_prompt(qb.reshape(shp), kb.reshape(shp), vb.reshape(shp), bias2, n_heads_a,
                               t_attn, n_sub).reshape(mp, w_a)
            qb, kb, vb, kf, vf = _qkv_proj(xs_f, w_in_a_b, j, q_scale, ms, TN_QKV)
            outs_s["k"].append(to_batch_major(kf).reshape(bs, t_new, n_heads_a, dh_a))
            outs_s["v"].append(to_batch_major(vf).reshape(bs, t_new, n_heads_a, dh_a))
            q8 = to_batch_major(qb, SUBLANES).reshape(bs, SUBLANES, n_heads_a, dh_a)
            q_hq = q8.transpose(0, 2, 1, 3).reshape(bs, n_heads_a * SUBLANES, dh_a)
            bias_cols = jnp.repeat(bias2, SUBLANES)[:, None]
            n_layers_a, pool = cache_k.shape[0], cache_k.shape[1]
            as_stored = lambda c: c.reshape(n_layers_a * pool, page * n_heads_a, dh_a)
            o_s = _sb_sample(page_table, q_hq, bias_cols, to_batch_major(kb, page), to_batch_major(vb, page),
                             as_stored(cache_k), as_stored(cache_v), j * pool, n_heads_a, t_new,
                             _pick(page_table.shape[1], PAGES_PER_STEP))
            o_s = o_s.reshape(bs, n_heads_a, SUBLANES, dh_a).transpose(0, 2, 1, 3).reshape(bs, SUBLANES, w_a)
            mix_s = to_time_major(o_s).astype(BF16)
        else:
            w_out = w_out_b_b
            n_main = 2 * qk_w + 2 * v_w
            w_gate = jnp.pad(w_in_b_b[j, :, n_main:], ((0, 0), (0, LANES - 2 * n_heads_b)))
            b_gate = jnp.pad(b_gate_b[j], (0, LANES - 2 * n_heads_b))[None, :]
            g_head = g_head_b[j][None, :]
            q, k, v, o, g = _mlstm_proj(xp_b, w_in_b_b, j, w_gate, qk_w, v_w, tm_p, 4)
            r3 = lambda a: a.reshape(bp, seq, -1)
            l = MLSTM_CHUNK if seq % MLSTM_CHUNK == 0 else seq
            mix_p, c, n, m = _mlstm(r3(q), r3(k), r3(v), r3(o), r3(g), b_gate, g_head, n_heads_b, l, l)
            mix_p = mix_p.reshape(mp, v_w)
            outs_p["c"].append(c)
            outs_p["n"].append(n)
            outs_p["m"].append(m[:, :, 0])
            q, k, v, o, g = _mlstm_proj(xs_b, w_in_b_b, j, w_gate, qk_w, v_w, ms, 4)
            ls = MLSTM_CHUNK if t_new % MLSTM_CHUNK == 0 else t_new
            assert ls == t_new and t_new <= SAMPLE_CHUNK
            padl = lambda a: to_batch_major(a, SAMPLE_CHUNK)
            m0 = jnp.broadcast_to(state_m[j][:, :, None], (bs, n_heads_b, LANES))
            y, c, n, m = _mlstm(padl(q), padl(k), padl(v), padl(o), padl(g), b_gate, g_head, n_heads_b,
                                SAMPLE_CHUNK, t_new, init=(state_C[j], state_n[j], m0))
            mix_s = to_time_major(y)
            outs_s["c"].append(c)
            outs_s["n"].append(n)
            outs_s["m"].append(m[:, :, 0])

        g1, b1 = ln1_g[i][None, :], ln1_b[i][None, :]
        g2, b2 = ln2_g[i][None, :], ln2_b[i][None, :]
        cw, cb = conv_w[i], conv_b[i][None, :]

        xp_f, xp_b = _proj_ln(mix_p, w_out, j, xp_f, g1, b1, alpha, _pick(seq, TM_LN))
        tm_f = _pick(seq, TM_FFN)
        gp, conv_p = _ffn_up_prompt(xp_b, w_up, i, cw, cb, bp, tm_f, tn_f, _pick(tm_f, FFN_ROWS))
        outs_p["conv"].append(conv_p)
        xp_f, xp_b = _proj_ln(gp, w_down_b, i, xp_f, g2, b2, alpha, _pick(seq, TM_DOWN))

        xs_f, xs_b = _proj_ln(mix_s, w_out, j, xs_f, g1, b1, alpha, ms)
        gs, conv_s = _ffn_up_sample(xs_b, w_up, i, cw, cb, state_conv[i].transpose(1, 0, 2), t_new, tn_f)
        outs_s["conv"].append(conv_s.transpose(1, 0, 2))
        xs_f, xs_b = _proj_ln(gs, w_down_b, i, xs_f, g2, b2, alpha, ms)

    y_prompt = xp_f.reshape(bp, seq, d_model)
    y_sample = xs_f.reshape(t_new, bs, d_model).transpose(1, 0, 2)
    st = jnp.stack
    return (y_prompt, y_sample, st(outs_p["k"]), st(outs_p["v"]), st(outs_s["k"]), st(outs_s["v"]),
            st(outs_p["c"]), st(outs_p["n"]), st(outs_p["m"]), st(outs_s["c"]), st(outs_s["n"]), st(outs_s["m"]),
            st(outs_p["conv"]), st(outs_s["conv"]))
```
